```python
import jax, jax.numpy as jnp
from jax import lax
import numpy as np

D_MODEL = 1024
BATCH = 8
SEQ = 8192
DEPTH = 4
DEC_BATCH = 8
DEC_SEQ = 64
PAST_LEN = 1024

CHUNK = 64
N_META = 16
N_MIXERS = 2
N_POOL_LAYERS = (DEPTH + 1) // 2
N_RET_LAYERS = DEPTH // 2
POOL_WINDOWS = (2, 4, 8, 16)
N_POOL_GROUPS = len(POOL_WINDOWS)
POOL_GROUP = D_MODEL // N_POOL_GROUPS
POOL_HIST = max(POOL_WINDOWS) - 1
RET_HEADS = 4
RET_DK = D_MODEL // RET_HEADS
RET_DV = 2 * D_MODEL // RET_HEADS
RET_VDIM = RET_HEADS * RET_DV
ROPE_BASE = 10000.0
D_FF = 2816
EPS = 1e-6

kernel_name = "meta_pool_retention_macaron_stream_step"


def rms_norm(x, g):
    xf = x.astype(jnp.float32)
    y = xf * lax.rsqrt(jnp.mean(xf * xf, axis=-1, keepdims=True) + EPS)
    return (y * g.astype(jnp.float32)).astype(x.dtype)


def swiglu(u, w_in, w_out):
    a, b = jnp.split(u @ w_in, 2, axis=-1)
    return (jax.nn.silu(a) * b) @ w_out


def pool_mixer(u, hist, hist_valid, w_pool, scale):
    B, n, D = u.shape
    P = POOL_HIST
    full = jnp.concatenate([hist.astype(u.dtype), u], axis=1)
    new_hist = full[:, -P:]
    cs = jnp.concatenate([jnp.zeros((B, 1, D), jnp.float32),
                          jnp.cumsum(full.astype(jnp.float32), axis=1)], axis=1)
    valid = jnp.concatenate([jnp.full((P,), hist_valid, jnp.float32), jnp.ones((n,), jnp.float32)])
    cv = jnp.concatenate([jnp.zeros((1,), jnp.float32), jnp.cumsum(valid)])
    means = []
    for gi, w in enumerate(POOL_WINDOWS):
        c0, c1 = gi * POOL_GROUP, (gi + 1) * POOL_GROUP
        s = cs[:, P + 1:, c0:c1] - cs[:, P + 1 - w:P + 1 - w + n, c0:c1]
        cnt = cv[P + 1:] - cv[P + 1 - w:P + 1 - w + n]
        means.append(s / cnt[None, :, None])
    pooled = (jnp.concatenate(means, axis=-1) - u.astype(jnp.float32)).astype(u.dtype)
    pooled = pooled.reshape(B, n, N_POOL_GROUPS, POOL_GROUP)
    y = jnp.einsum('bngc,gcd->bngd', pooled, w_pool).reshape(B, n, D)
    return y * scale, new_hist


def rotary(x, pos):
    half = RET_DK // 2
    inv = ROPE_BASE ** (-jnp.arange(half, dtype=jnp.float32) / half)
    ang = pos.astype(jnp.float32)[:, None] * inv[None, :]
    cos = jnp.cos(ang)[None, :, None, :].astype(x.dtype)
    sin = jnp.sin(ang)[None, :, None, :].astype(x.dtype)
    x1, x2 = x[..., :half], x[..., half:]
    return jnp.concatenate([x1 * cos - x2 * sin, x1 * sin + x2 * cos], axis=-1)


def log_gamma():
    return jnp.log1p(-(2.0 ** (-5.0 - jnp.arange(RET_HEADS, dtype=jnp.float32))))


def retention_block(q, k, v, S):
    n = q.shape[1]
    lg = log_gamma()
    idx = jnp.arange(n, dtype=jnp.float32)
    diff = idx[:, None] - idx[None, :]
    dmask = jnp.where(diff[None] >= 0,
                      jnp.exp(jnp.maximum(diff, 0.0)[None] * lg[:, None, None]),
                      0.0).astype(q.dtype)
    scores = jnp.einsum('bihd,bjhd->bhij', q, k) * dmask[None]
    intra = jnp.einsum('bhij,bjhe->bihe', scores, v)
    qdec = jnp.exp((idx + 1.0)[:, None] * lg[None, :]).astype(q.dtype)
    cross = jnp.einsum('bihd,bhde->bihe', q * qdec[None, :, :, None], S)
    kdec = jnp.exp((n - 1.0 - idx)[:, None] * lg[None, :]).astype(q.dtype)
    s_dec = jnp.exp(n * lg).astype(S.dtype)[None, :, None, None]
    S_new = s_dec * S + jnp.einsum('bjhd,bjhe->bhde', k * kdec[None, :, :, None], v)
    return intra + cross, S_new


def retention_mixer(u, pos, S0, w_in, w_out, norm_g):
    B, n, D = u.shape
    proj = u @ w_in
    q, k, v, g = jnp.split(proj, [D, 2 * D, 2 * D + RET_VDIM], axis=-1)
    q = rotary(q.reshape(B, n, RET_HEADS, RET_DK), pos)
    k = rotary(k.reshape(B, n, RET_HEADS, RET_DK), pos) * (RET_DK ** -0.5)
    v = v.reshape(B, n, RET_HEADS, RET_DV)
    if S0 is None:
        pad = (-n) % CHUNK
        nc = (n + pad) // CHUNK

        def to_blocks(t):
            t = jnp.pad(t, ((0, 0), (pad, 0), (0, 0), (0, 0)))
            return jnp.moveaxis(t.reshape(B, nc, CHUNK, RET_HEADS, t.shape[-1]), 1, 0)

        def step(S, qkv):
            o, S2 = retention_block(qkv[0], qkv[1], qkv[2], S)
            return S2, o

        S_init = jnp.zeros((B, RET_HEADS, RET_DK, RET_DV), q.dtype)
        S_fin, o = lax.scan(step, S_init, (to_blocks(q), to_blocks(k), to_blocks(v)))
        o = jnp.moveaxis(o, 0, 1).reshape(B, nc * CHUNK, RET_HEADS, RET_DV)[:, pad:]
    else:
        o, S_fin = retention_block(q, k, v, S0.astype(q.dtype))
    of = o.astype(jnp.float32)
    of = of * lax.rsqrt(jnp.mean(of * of, axis=-1, keepdims=True) + EPS)
    o = (of.reshape(B, n, RET_VDIM) * norm_g.astype(jnp.float32)).astype(u.dtype)
    y = (jax.nn.silu(g) * o) @ w_out
    return y, S_fin


def setup_inputs(seed: int = 0) -> dict:
    key = jax.random.key(seed)
    ks = jax.random.split(key, 20)
    f32 = jnp.float32
    nrm = lambda k, shape, s: jax.random.normal(k, shape, f32) * s
    return {
        "x_prompt": nrm(ks[0], (BATCH, SEQ, D_MODEL), 1.0),
        "x_sample": nrm(ks[1], (DEC_BATCH, DEC_SEQ, D_MODEL), 1.0),
        "cache_pool": nrm(ks[2], (N_POOL_LAYERS, DEC_BATCH, POOL_HIST, D_MODEL), 1.0),
        "state_ret": nrm(ks[3], (N_RET_LAYERS, DEC_BATCH, RET_HEADS, RET_DK, RET_DV), 0.5),
        "meta_tokens": nrm(ks[4], (N_META, D_MODEL), 1.0),
        "norm_g": 1.0 + nrm(ks[5], (DEPTH, 3, D_MODEL), 0.02),
        "final_norm_g": 1.0 + nrm(ks[6], (D_MODEL,), 0.02),
        "w_ffn1_in": nrm(ks[7], (DEPTH, D_MODEL, 2 * D_FF), D_MODEL ** -0.5),
        "w_ffn1_out": nrm(ks[8], (DEPTH, D_FF, D_MODEL), D_FF ** -0.5),
        "w_ffn2_in": nrm(ks[9], (DEPTH, D_MODEL, 2 * D_FF), D_MODEL ** -0.5),
        "w_ffn2_out": nrm(ks[10], (DEPTH, D_FF, D_MODEL), D_FF ** -0.5),
        "w_pool": nrm(ks[11], (N_POOL_LAYERS, N_POOL_GROUPS, POOL_GROUP, POOL_GROUP), POOL_GROUP ** -0.5),
        "pool_scale": 1.0 + nrm(ks[12], (N_POOL_LAYERS, D_MODEL), 0.02),
        "w_ret_in": nrm(ks[13], (N_RET_LAYERS, D_MODEL, 2 * D_MODEL + 2 * RET_VDIM), D_MODEL ** -0.5),
        "w_ret_out": nrm(ks[14], (N_RET_LAYERS, RET_VDIM, D_MODEL), RET_VDIM ** -0.5),
        "ret_norm_g": 1.0 + nrm(ks[15], (N_RET_LAYERS, RET_VDIM), 0.02),
    }


def reference(x_prompt, x_sample, cache_pool, state_ret, meta_tokens, norm_g, final_norm_g,
              w_ffn1_in, w_ffn1_out, w_ffn2_in, w_ffn2_out, w_pool, pool_scale,
              w_ret_in, w_ret_out, ret_norm_g):
    B = x_prompt.shape[0]
    meta = jnp.broadcast_to(meta_tokens[None].astype(x_prompt.dtype), (B, N_META, D_MODEL))
    xp = jnp.concatenate([meta, x_prompt], axis=1)
    xs = x_sample
    pos_p = jnp.arange(xp.shape[1])
    pos_s = N_META + PAST_LEN + jnp.arange(xs.shape[1])
    pool_p, pool_s, ret_p, ret_s = [], [], [], []
    for i in range(DEPTH):
        g = norm_g[i]
        xp = xp + 0.5 * swiglu(rms_norm(xp, g[0]), w_ffn1_in[i], w_ffn1_out[i])
        xs = xs + 0.5 * swiglu(rms_norm(xs, g[0]), w_ffn1_in[i], w_ffn1_out[i])
        up = rms_norm(xp, g[1])
        us = rms_norm(xs, g[1])
        j = i // N_MIXERS
        if i % N_MIXERS == 0:
            hist0 = jnp.zeros((B, POOL_HIST, D_MODEL), up.dtype)
            mp, hp = pool_mixer(up, hist0, 0.0, w_pool[j], pool_scale[j])
            ms, hs = pool_mixer(us, cache_pool[j], 1.0, w_pool[j], pool_scale[j])
            pool_p.append(hp)
            pool_s.append(hs)
        else:
            mp, sp = retention_mixer(up, pos_p, None, w_ret_in[j], w_ret_out[j], ret_norm_g[j])
            ms, ss = retention_mixer(us, pos_s, state_ret[j], w_ret_in[j], w_ret_out[j], ret_norm_g[j])
            ret_p.append(sp)
            ret_s.append(ss)
        xp = xp + mp
        xs = xs + ms
        xp = xp + 0.5 * swiglu(rms_norm(xp, g[2]), w_ffn2_in[i], w_ffn2_out[i])
        xs = xs + 0.5 * swiglu(rms_norm(xs, g[2]), w_ffn2_in[i], w_ffn2_out[i])
    y_prompt = rms_norm(xp, final_norm_g)[:, N_META:]
    y_sample = rms_norm(xs, final_norm_g)
    return (y_prompt, y_sample, jnp.stack(pool_p), jnp.stack(pool_s), jnp.stack(ret_p), jnp.stack(ret_s))
```

```python
import functools
import math

import jax
import jax.numpy as jnp
from jax import lax
from jax.experimental import pallas as pl
from jax.experimental.pallas import tpu as pltpu

F32 = jnp.float32
BF16 = jnp.bfloat16

D_MODEL = 1024
D_FF = 2816
EPS = 1e-6
N_META_ROWS = 16
PAST_LEN = 1024

POOL_WINDOWS = (2, 4, 8, 16)
POOL_GROUP = D_MODEL // len(POOL_WINDOWS)
POOL_HIST = max(POOL_WINDOWS) - 1
HIST_ROWS = POOL_HIST + 1

RET_HEADS = 4
RET_DK = D_MODEL // RET_HEADS
RET_DV = 2 * D_MODEL // RET_HEADS
RET_VDIM = RET_HEADS * RET_DV
ROPE_BASE = 10000.0
ROPE_HALF = RET_DK // 2
LOG_GAMMA = tuple(math.log1p(-(2.0 ** (-5.0 - h))) for h in range(RET_HEADS))

VMEM_LIMIT_BYTES = 56 * 1024 * 1024

FFN_TILE = 512
FFN_CHUNK = 256
POOL_TILE = 1024
RET_TILE = 512
RET_CHUNK = 256
PROJ_CHUNK = 256


def _rms(x, gain):
    return x * lax.rsqrt(jnp.mean(x * x, axis=-1, keepdims=True) + EPS) * gain


def _dot(a, b):
    return jnp.dot(a, b, preferred_element_type=F32)


def _ffn_body(x_ref, g_ref, win_ref, wout_ref, o_ref, h_ref):
    u = _rms(x_ref[...], g_ref[...]).astype(BF16)
    for c in range(D_FF // FFN_CHUNK):
        lo = c * FFN_CHUNK
        a = _dot(u, win_ref[:, lo:lo + FFN_CHUNK])
        b = _dot(u, win_ref[:, D_FF + lo:D_FF + lo + FFN_CHUNK])
        h_ref[:, lo:lo + FFN_CHUNK] = (a * jax.nn.sigmoid(a) * b).astype(BF16)
    y = _dot(h_ref[...], wout_ref[...])
    o_ref[...] = x_ref[...] + 0.5 * y


def _ffn(x_all, gain, w_in, w_out):
    rows = x_all.shape[0]
    return pl.pallas_call(
        _ffn_body,
        grid=(rows // FFN_TILE,),
        in_specs=[
            pl.BlockSpec((FFN_TILE, D_MODEL), lambda i: (i, 0)),
            pl.BlockSpec((1, D_MODEL), lambda i: (0, 0)),
            pl.BlockSpec((D_MODEL, 2 * D_FF), lambda i: (0, 0), pipeline_mode=pl.Buffered(1)),
            pl.BlockSpec((D_FF, D_MODEL), lambda i: (0, 0), pipeline_mode=pl.Buffered(1)),
        ],
        out_specs=pl.BlockSpec((FFN_TILE, D_MODEL), lambda i: (i, 0)),
        out_shape=jax.ShapeDtypeStruct(x_all.shape, x_all.dtype),
        scratch_shapes=[pltpu.VMEM((FFN_TILE, D_FF), BF16)],
        input_output_aliases={0: 0},
        compiler_params=pltpu.CompilerParams(
            dimension_semantics=("arbitrary",), vmem_limit_bytes=VMEM_LIMIT_BYTES),
        name="ffn",
    )(x_all, gain, w_in, w_out)


def _pool_body(x_ref, hist_ref, g_ref, w_ref, scale_ref, o_ref, newhist_ref, ext_ref,
               *, tile, hist_valid):
    t = pl.program_id(1)

    @pl.when(t == 0)
    def _():
        ext_ref[0:HIST_ROWS, :] = hist_ref[0]

    x = x_ref[...]
    u = _rms(x, g_ref[...])
    ext_ref[HIST_ROWS:HIST_ROWS + tile, :] = u

    if not hist_valid:
        pos = t * tile + lax.broadcasted_iota(jnp.int32, (tile, 1), 0)

    for gi, w in enumerate(POOL_WINDOWS):
        cols = slice(gi * POOL_GROUP, (gi + 1) * POOL_GROUP)
        s = ext_ref[:, cols]
        k = 1
        while k < w:
            s = s + pltpu.roll(s, k, 0)
            k *= 2
        s = s[HIST_ROWS:, :]
        if hist_valid:
            mean = s * (1.0 / w)
        else:
            mean = s / jnp.minimum(pos + 1, w).astype(F32)
        pooled = (mean - u[:, cols]).astype(BF16)
        y = _dot(pooled, w_ref[gi]) * scale_ref[:, cols]
        o_ref[:, cols] = x[:, cols] + y

    last_rows = ext_ref[tile:tile + HIST_ROWS, :]
    ext_ref[0:HIST_ROWS, :] = last_rows

    @pl.when(t == pl.num_programs(1) - 1)
    def _():
        newhist_ref[0] = last_rows


def _pool(x_all, hist, gain, w_pool, scale, *, row0, n_streams, length, tile, hist_valid,
          shared_hist):
    tiles = length // tile
    blk0 = row0 // tile
    hist_map = (lambda b, t: (0, 0, 0)) if shared_hist else (lambda b, t: (b, 0, 0))
    return pl.pallas_call(
        functools.partial(_pool_body, tile=tile, hist_valid=hist_valid),
        grid=(n_streams, tiles),
        in_specs=[
            pl.BlockSpec((tile, D_MODEL), lambda b, t: (blk0 + b * tiles + t, 0)),
            pl.BlockSpec((1, HIST_ROWS, D_MODEL), hist_map),
            pl.BlockSpec((1, D_MODEL), lambda b, t: (0, 0)),
            pl.BlockSpec(w_pool.shape, lambda b, t: (0, 0, 0)),
            pl.BlockSpec((1, D_MODEL), lambda b, t: (0, 0)),
        ],
        out_specs=[
            pl.BlockSpec((tile, D_MODEL), lambda b, t: (blk0 + b * tiles + t, 0)),
            pl.BlockSpec((1, HIST_ROWS, D_MODEL), lambda b, t: (b, 0, 0)),
        ],
        out_shape=[
            jax.ShapeDtypeStruct(x_all.shape, x_all.dtype),
            jax.ShapeDtypeStruct((n_streams, HIST_ROWS, D_MODEL), F32),
        ],
        scratch_shapes=[pltpu.VMEM((tile + HIST_ROWS, D_MODEL), F32)],
        input_output_aliases={0: 0},
        compiler_params=pltpu.CompilerParams(
            dimension_semantics=("arbitrary", "arbitrary"), vmem_limit_bytes=VMEM_LIMIT_BYTES),
        name="pool",
    )(x_all, hist, gain, w_pool, scale)


def _ret_body(x_ref, cos_ref, sin_ref, s0_ref, g_ref, win_ref, wout_ref, ng_ref,
              o_ref, sfin_ref,
              qd_ref, kd_ref, v_ref, gate_ref, og_ref, s_ref, dq_ref, dk_ref,
              *, tile, chunk):
    b = pl.program_id(0)
    t = pl.program_id(1)
    n_chunks = tile // chunk

    @pl.when((b == 0) & (t == 0))
    def _():
        steps = (lax.broadcasted_iota(jnp.int32, (chunk, ROPE_HALF), 0) + 1).astype(F32)
        for h in range(RET_HEADS):
            dq_ref[h] = jnp.exp(steps * LOG_GAMMA[h])
            dk_ref[h] = jnp.exp(steps * (-LOG_GAMMA[h])) * (RET_DK ** -0.5)

    @pl.when(t == 0)
    def _():
        s_ref[...] = s0_ref[0]

    u = _rms(x_ref[...], g_ref[...]).astype(BF16)
    cos = cos_ref[...]
    sin = sin_ref[...]

    def rotate(p):
        x1 = p[:, :ROPE_HALF]
        x2 = p[:, ROPE_HALF:]
        return x1 * cos - x2 * sin, x1 * sin + x2 * cos

    for h in range(RET_HEADS):
        lo = h * RET_DK
        for off, dec_ref, dst_ref in ((0, dq_ref, qd_ref), (D_MODEL, dk_ref, kd_ref)):
            r1, r2 = rotate(_dot(u, win_ref[:, off + lo:off + lo + RET_DK]))
            dec = dec_ref[h]
            for ci in range(n_chunks):
                rows = slice(ci * chunk, (ci + 1) * chunk)
                dst_ref[rows, lo:lo + ROPE_HALF] = (r1[rows] * dec).astype(BF16)
                dst_ref[rows, lo + ROPE_HALF:lo + RET_DK] = (r2[rows] * dec).astype(BF16)

    for c in range(RET_VDIM // PROJ_CHUNK):
        lo = c * PROJ_CHUNK
        v_ref[:, lo:lo + PROJ_CHUNK] = _dot(
            u, win_ref[:, 2 * D_MODEL + lo:2 * D_MODEL + lo + PROJ_CHUNK]).astype(BF16)
        gp = _dot(u, win_ref[:, 2 * D_MODEL + RET_VDIM + lo:2 * D_MODEL + RET_VDIM + lo + PROJ_CHUNK])
        gate_ref[:, lo:lo + PROJ_CHUNK] = (gp * jax.nn.sigmoid(gp)).astype(BF16)

    causal = (lax.broadcasted_iota(jnp.int32, (chunk, chunk), 0)
              >= lax.broadcasted_iota(jnp.int32, (chunk, chunk), 1))
    for ci in range(n_chunks):
        rows = slice(ci * chunk, (ci + 1) * chunk)
        for h in range(RET_HEADS):
            kcols = slice(h * RET_DK, (h + 1) * RET_DK)
            vcols = slice(h * RET_DV, (h + 1) * RET_DV)
            qd = qd_ref[rows, kcols]
            kd = kd_ref[rows, kcols]
            v = v_ref[rows, vcols]
            scores = lax.dot_general(qd, kd, (((1,), (1,)), ((), ())), preferred_element_type=F32)
            p = jnp.where(causal, scores, 0.0).astype(BF16)
            state = s_ref[h]
            o = _dot(p, v) + _dot(qd, state.astype(BF16))
            kv = lax.dot_general(kd, v, (((0,), (0,)), ((), ())), preferred_element_type=F32)
            s_ref[h] = math.exp(chunk * LOG_GAMMA[h]) * (state + kv)
            on = o * lax.rsqrt(jnp.mean(o * o, axis=-1, keepdims=True) + EPS) * ng_ref[:, vcols]
            og_ref[rows, vcols] = (gate_ref[rows, vcols].astype(F32) * on).astype(BF16)

    o_ref[...] = x_ref[...] + _dot(og_ref[...], wout_ref[...])

    @pl.when(t == pl.num_programs(1) - 1)
    def _():
        sfin_ref[0] = s_ref[...]


def _ret(x_all, cos, sin, s0, gain, w_in, w_out, norm_gain, *, row0, n_streams, length, tile,
         chunk, shared_state):
    tiles = length // tile
    blk0 = row0 // tile
    state_shape = (1, RET_HEADS, RET_DK, RET_DV)
    s0_map = (lambda b, t: (0, 0, 0, 0)) if shared_state else (lambda b, t: (b, 0, 0, 0))
    return pl.pallas_call(
        functools.partial(_ret_body, tile=tile, chunk=chunk),
        grid=(n_streams, tiles),
        in_specs=[
            pl.BlockSpec((tile, D_MODEL), lambda b, t: (blk0 + b * tiles + t, 0)),
            pl.BlockSpec((tile, ROPE_HALF), lambda b, t: (t, 0)),
            pl.BlockSpec((tile, ROPE_HALF), lambda b, t: (t, 0)),
            pl.BlockSpec(state_shape, s0_map),
            pl.BlockSpec((1, D_MODEL), lambda b, t: (0, 0)),
            pl.BlockSpec(w_in.shape, lambda b, t: (0, 0), pipeline_mode=pl.Buffered(1)),
            pl.BlockSpec(w_out.shape, lambda b, t: (0, 0), pipeline_mode=pl.Buffered(1)),
            pl.BlockSpec((1, RET_VDIM), lambda b, t: (0, 0)),
        ],
        out_specs=[
            pl.BlockSpec((tile, D_MODEL), lambda b, t: (blk0 + b * tiles + t, 0)),
            pl.BlockSpec(state_shape, lambda b, t: (b, 0, 0, 0)),
        ],
        out_shape=[
            jax.ShapeDtypeStruct(x_all.shape, x_all.dtype),
            jax.ShapeDtypeStruct((n_streams,) + state_shape[1:], F32),
        ],
        scratch_shapes=[
            pltpu.VMEM((tile, D_MODEL), BF16),
            pltpu.VMEM((tile, D_MODEL), BF16),
            pltpu.VMEM((tile, RET_VDIM), BF16),
            pltpu.VMEM((tile, RET_VDIM), BF16),
            pltpu.VMEM((tile, RET_VDIM), BF16),
            pltpu.VMEM((RET_HEADS, RET_DK, RET_DV), F32),
            pltpu.VMEM((RET_HEADS, chunk, ROPE_HALF), F32),
            pltpu.VMEM((RET_HEADS, chunk, ROPE_HALF), F32),
        ],
        input_output_aliases={0: 0},
        compiler_params=pltpu.CompilerParams(
            dimension_semantics=("arbitrary", "arbitrary"), vmem_limit_bytes=VMEM_LIMIT_BYTES),
        name="ret",
    )(x_all, cos, sin, s0, gain, w_in, w_out, norm_gain)


def _norm_body(x_ref, g_ref, o_ref):
    o_ref[...] = _rms(x_ref[...], g_ref[...])


def _final_norm(x_all, gain, *, row0, rows, tile):
    blk0 = row0 // tile
    return pl.pallas_call(
        _norm_body,
        grid=(rows // tile,),
        in_specs=[
            pl.BlockSpec((tile, D_MODEL), lambda i: (blk0 + i, 0)),
            pl.BlockSpec((1, D_MODEL), lambda i: (0, 0)),
        ],
        out_specs=pl.BlockSpec((tile, D_MODEL), lambda i: (i, 0)),
        out_shape=jax.ShapeDtypeStruct((rows, D_MODEL), x_all.dtype),
        compiler_params=pltpu.CompilerParams(
            dimension_semantics=("arbitrary",), vmem_limit_bytes=VMEM_LIMIT_BYTES),
        name="final_norm",
    )(x_all, gain)


def _rope_tables(pos0, length):
    inv = ROPE_BASE ** (-jnp.arange(ROPE_HALF, dtype=F32) / ROPE_HALF)
    ang = (pos0 + jnp.arange(length)).astype(F32)[:, None] * inv[None, :]
    return jnp.cos(ang), jnp.sin(ang)


def kernel(x_prompt, x_sample, cache_pool, state_ret, meta_tokens, norm_g, final_norm_g, w_ffn1_in, w_ffn1_out, w_ffn2_in, w_ffn2_out, w_pool, pool_scale, w_ret_in, w_ret_out, ret_norm_g):
    batch, seq, _ = x_prompt.shape
    dec_batch, dec_seq, _ = x_sample.shape
    depth = norm_g.shape[0]
    assert meta_tokens.shape[0] == N_META_ROWS
    assert seq % max(POOL_TILE, RET_TILE) == 0

    prompt_row0 = 0
    sample_row0 = batch * seq
    meta_row0 = sample_row0 + dec_batch * dec_seq
    used_rows = meta_row0 + N_META_ROWS
    rows = pl.cdiv(used_rows, FFN_TILE) * FFN_TILE
    assert sample_row0 % dec_seq == 0 and meta_row0 % N_META_ROWS == 0
    x_all = jnp.concatenate([
        x_prompt.reshape(batch * seq, D_MODEL),
        x_sample.reshape(dec_batch * dec_seq, D_MODEL),
        meta_tokens.astype(x_prompt.dtype),
        jnp.zeros((rows - used_rows, D_MODEL), x_prompt.dtype)], axis=0)

    groups = (
        dict(name="meta", row0=meta_row0, n=1, length=N_META_ROWS, pos0=0),
        dict(name="prompt", row0=prompt_row0, n=batch, length=seq, pos0=N_META_ROWS),
        dict(name="sample", row0=sample_row0, n=dec_batch, length=dec_seq,
             pos0=N_META_ROWS + PAST_LEN),
    )
    rope = {g["name"]: _rope_tables(g["pos0"], g["length"]) for g in groups}
    zero_hist = jnp.zeros((1, HIST_ROWS, D_MODEL), F32)
    zero_state = jnp.zeros((1, RET_HEADS, RET_DK, RET_DV), F32)

    pool_p, pool_s, ret_p, ret_s = [], [], [], []
    for i in range(depth):
        gains = norm_g[i].astype(F32)
        x_all = _ffn(x_all, gains[0:1], w_ffn1_in[i].astype(BF16), w_ffn1_out[i].astype(BF16))
        j = i // 2
        if i % 2 == 0:
            wp = w_pool[j].astype(BF16)
            sc = pool_scale[j].astype(F32)[None, :]
            cache = jnp.pad(cache_pool[j].astype(F32), ((0, 0), (1, 0), (0, 0)))
            x_all, hist_meta = _pool(x_all, zero_hist, gains[1:2], wp, sc, row0=meta_row0,
                                     n_streams=1, length=N_META_ROWS, tile=N_META_ROWS,
                                     hist_valid=False, shared_hist=True)
            x_all, hist_p = _pool(x_all, hist_meta, gains[1:2], wp, sc, row0=prompt_row0,
                                  n_streams=batch, length=seq, tile=POOL_TILE,
                                  hist_valid=True, shared_hist=True)
            x_all, hist_s = _pool(x_all, cache, gains[1:2], wp, sc, row0=sample_row0,
                                  n_streams=dec_batch, length=dec_seq, tile=dec_seq,
                                  hist_valid=True, shared_hist=False)
            pool_p.append(hist_p[:, 1:])
            pool_s.append(hist_s[:, 1:])
        else:
            wi = w_ret_in[j].astype(BF16)
            wo = w_ret_out[j].astype(BF16)
            ng = ret_norm_g[j].astype(F32)[None, :]
            x_all, state_meta = _ret(x_all, *rope["meta"], zero_state, gains[1:2], wi, wo, ng,
                                     row0=meta_row0, n_streams=1, length=N_META_ROWS,
                                     tile=N_META_ROWS, chunk=N_META_ROWS, shared_state=True)
            x_all, state_p = _ret(x_all, *rope["prompt"], state_meta, gains[1:2], wi, wo, ng,
                                  row0=prompt_row0, n_streams=batch, length=seq,
                                  tile=RET_TILE, chunk=RET_CHUNK, shared_state=True)
            x_all, state_s = _ret(x_all, *rope["sample"], state_ret[j].astype(F32), gains[1:2],
                                  wi, wo, ng, row0=sample_row0, n_streams=dec_batch,
                                  length=dec_seq, tile=dec_seq, chunk=dec_seq, shared_state=False)
            ret_p.append(state_p)
            ret_s.append(state_s)
        x_all = _ffn(x_all, gains[2:3], w_ffn2_in[i].astype(BF16), w_ffn2_out[i].astype(BF16))

    fg = final_norm_g.astype(F32)[None, :]
    y_prompt = _final_norm(x_all, fg, row0=prompt_row0, rows=batch * seq, tile=FFN_TILE)
    y_sample = _final_norm(x_all, fg, row0=sample_row0, rows=dec_batch * dec_seq,
                           tile=dec_batch * dec_seq)
    return (y_prompt.reshape(batch, seq, D_MODEL),
            y_sample.reshape(dec_batch, dec_seq, D_MODEL),
            jnp.stack(pool_p), jnp.stack(pool_s), jnp.stack(ret_p), jnp.stack(ret_s))
```

```python
import functools
import math

import jax
import jax.numpy as jnp
from jax import lax
from jax.experimental import pallas as pl
from jax.experimental.pallas import tpu as pltpu

F32 = jnp.float32
BF16 = jnp.bfloat16

D_MODEL = 1024
D_FF = 2816
EPS = 1e-6
N_META_ROWS = 16
PAST_LEN = 1024

POOL_WINDOWS = (2, 4, 8, 16)
POOL_GROUP = D_MODEL // len(POOL_WINDOWS)
POOL_HIST = max(POOL_WINDOWS) - 1
HIST_ROWS = POOL_HIST + 1

RET_HEADS = 4
RET_DK = D_MODEL // RET_HEADS
RET_DV = 2 * D_MODEL // RET_HEADS
RET_VDIM = RET_HEADS * RET_DV
ROPE_BASE = 10000.0
ROPE_HALF = RET_DK // 2
LOG_GAMMA = tuple(math.log1p(-(2.0 ** (-5.0 - h))) for h in range(RET_HEADS))

VMEM_LIMIT_BYTES = 56 * 1024 * 1024

FFN_TILE = 1024
FFN_CHUNK = 256
POOL_TILE = 1024
RET_TILE = 512
RET_CHUNK = 256
PROJ_CHUNK = 256


def _rms(x, gain):
    return x * lax.rsqrt(jnp.mean(x * x, axis=-1, keepdims=True) + EPS) * gain


def _dot(a, b):
    return jnp.dot(a, b, preferred_element_type=F32)


def _ffn_body(x_ref, g_ref, win_ref, wout_ref, o_ref, h_ref):
    u = _rms(x_ref[...], g_ref[...]).astype(BF16)
    for c in range(D_FF // FFN_CHUNK):
        lo = c * FFN_CHUNK
        a = _dot(u, win_ref[:, lo:lo + FFN_CHUNK])
        b = _dot(u, win_ref[:, D_FF + lo:D_FF + lo + FFN_CHUNK])
        h_ref[:, lo:lo + FFN_CHUNK] = (a * jax.nn.sigmoid(a) * b).astype(BF16)
    y = _dot(h_ref[...], wout_ref[...])
    o_ref[...] = x_ref[...] + 0.5 * y


def _ffn(x_all, gain, w_in, w_out):
    rows = x_all.shape[0]
    return pl.pallas_call(
        _ffn_body,
        grid=(rows // FFN_TILE,),
        in_specs=[
            pl.BlockSpec((FFN_TILE, D_MODEL), lambda i: (i, 0)),
            pl.BlockSpec((1, D_MODEL), lambda i: (0, 0)),
            pl.BlockSpec((D_MODEL, 2 * D_FF), lambda i: (0, 0), pipeline_mode=pl.Buffered(1)),
            pl.BlockSpec((D_FF, D_MODEL), lambda i: (0, 0), pipeline_mode=pl.Buffered(1)),
        ],
        out_specs=pl.BlockSpec((FFN_TILE, D_MODEL), lambda i: (i, 0)),
        out_shape=jax.ShapeDtypeStruct(x_all.shape, x_all.dtype),
        scratch_shapes=[pltpu.VMEM((FFN_TILE, D_FF), BF16)],
        input_output_aliases={0: 0},
        compiler_params=pltpu.CompilerParams(
            dimension_semantics=("arbitrary",), vmem_limit_bytes=VMEM_LIMIT_BYTES),
        name="ffn",
    )(x_all, gain, w_in, w_out)


def _pool_body(x_ref, hist_ref, g_ref, w_ref, scale_ref, o_ref, newhist_ref, ext_ref,
               *, tile, hist_valid):
    t = pl.program_id(1)

    @pl.when(t == 0)
    def _():
        ext_ref[0:HIST_ROWS, :] = hist_ref[0]

    x = x_ref[...]
    u = _rms(x, g_ref[...])
    ext_ref[HIST_ROWS:HIST_ROWS + tile, :] = u

    if not hist_valid:
        pos = t * tile + lax.broadcasted_iota(jnp.int32, (tile, 1), 0)

    for gi, w in enumerate(POOL_WINDOWS):
        cols = slice(gi * POOL_GROUP, (gi + 1) * POOL_GROUP)
        s = ext_ref[:, cols]
        k = 1
        while k < w:
            s = s + pltpu.roll(s, k, 0)
            k *= 2
        s = s[HIST_ROWS:, :]
        if hist_valid:
            mean = s * (1.0 / w)
        else:
            mean = s / jnp.minimum(pos + 1, w).astype(F32)
        pooled = (mean - u[:, cols]).astype(BF16)
        y = _dot(pooled, w_ref[gi]) * scale_ref[:, cols]
        o_ref[:, cols] = x[:, cols] + y

    last_rows = ext_ref[tile:tile + HIST_ROWS, :]
    ext_ref[0:HIST_ROWS, :] = last_rows

    @pl.when(t == pl.num_programs(1) - 1)
    def _():
        newhist_ref[0] = last_rows


def _pool(x_all, hist, gain, w_pool, scale, *, row0, n_streams, length, tile, hist_valid,
          shared_hist):
    tiles = length // tile
    blk0 = row0 // tile
    hist_map = (lambda b, t: (0, 0, 0)) if shared_hist else (lambda b, t: (b, 0, 0))
    return pl.pallas_call(
        functools.partial(_pool_body, tile=tile, hist_valid=hist_valid),
        grid=(n_streams, tiles),
        in_specs=[
            pl.BlockSpec((tile, D_MODEL), lambda b, t: (blk0 + b * tiles + t, 0)),
            pl.BlockSpec((1, HIST_ROWS, D_MODEL), hist_map),
            pl.BlockSpec((1, D_MODEL), lambda b, t: (0, 0)),
            pl.BlockSpec(w_pool.shape, lambda b, t: (0, 0, 0)),
            pl.BlockSpec((1, D_MODEL), lambda b, t: (0, 0)),
        ],
        out_specs=[
            pl.BlockSpec((tile, D_MODEL), lambda b, t: (blk0 + b * tiles + t, 0)),
            pl.BlockSpec((1, HIST_ROWS, D_MODEL), lambda b, t: (b, 0, 0)),
        ],
        out_shape=[
            jax.ShapeDtypeStruct(x_all.shape, x_all.dtype),
            jax.ShapeDtypeStruct((n_streams, HIST_ROWS, D_MODEL), F32),
        ],
        scratch_shapes=[pltpu.VMEM((tile + HIST_ROWS, D_MODEL), F32)],
        input_output_aliases={0: 0},
        compiler_params=pltpu.CompilerParams(
            dimension_semantics=("arbitrary", "arbitrary"), vmem_limit_bytes=VMEM_LIMIT_BYTES),
        name="pool",
    )(x_all, hist, gain, w_pool, scale)


def _ret_body(x_ref, cos_ref, sin_ref, s0_ref, g_ref, win_ref, wout_ref, ng_ref,
              o_ref, sfin_ref,
              qd_ref, kd_ref, v_ref, gate_ref, og_ref, s_ref, dq_ref, dk_ref,
              *, tile, chunk):
    b = pl.program_id(0)
    t = pl.program_id(1)
    n_chunks = tile // chunk

    @pl.when((b == 0) & (t == 0))
    def _():
        steps = (lax.broadcasted_iota(jnp.int32, (chunk, ROPE_HALF), 0) + 1).astype(F32)
        for h in range(RET_HEADS):
            dq_ref[h] = jnp.exp(steps * LOG_GAMMA[h])
            dk_ref[h] = jnp.exp(steps * (-LOG_GAMMA[h])) * (RET_DK ** -0.5)

    @pl.when(t == 0)
    def _():
        s_ref[...] = s0_ref[0]

    u = _rms(x_ref[...], g_ref[...]).astype(BF16)
    cos = cos_ref[...]
    sin = sin_ref[...]

    def rotate(p):
        x1 = p[:, :ROPE_HALF]
        x2 = p[:, ROPE_HALF:]
        return x1 * cos - x2 * sin, x1 * sin + x2 * cos

    for h in range(RET_HEADS):
        lo = h * RET_DK
        for off, dec_ref, dst_ref in ((0, dq_ref, qd_ref), (D_MODEL, dk_ref, kd_ref)):
            r1, r2 = rotate(_dot(u, win_ref[:, off + lo:off + lo + RET_DK]))
            dec = dec_ref[h]
            for ci in range(n_chunks):
                rows = slice(ci * chunk, (ci + 1) * chunk)
                dst_ref[rows, lo:lo + ROPE_HALF] = (r1[rows] * dec).astype(BF16)
                dst_ref[rows, lo + ROPE_HALF:lo + RET_DK] = (r2[rows] * dec).astype(BF16)

    for c in range(RET_VDIM // PROJ_CHUNK):
        lo = c * PROJ_CHUNK
        v_ref[:, lo:lo + PROJ_CHUNK] = _dot(
            u, win_ref[:, 2 * D_MODEL + lo:2 * D_MODEL + lo + PROJ_CHUNK]).astype(BF16)
        gp = _dot(u, win_ref[:, 2 * D_MODEL + RET_VDIM + lo:2 * D_MODEL + RET_VDIM + lo + PROJ_CHUNK])
        gate_ref[:, lo:lo + PROJ_CHUNK] = (gp * jax.nn.sigmoid(gp)).astype(BF16)

    causal = (lax.broadcasted_iota(jnp.int32, (chunk, chunk), 0)
              >= lax.broadcasted_iota(jnp.int32, (chunk, chunk), 1))
    for ci in range(n_chunks):
        rows = slice(ci * chunk, (ci + 1) * chunk)
        for h in range(RET_HEADS):
            kcols = slice(h * RET_DK, (h + 1) * RET_DK)
            vcols = slice(h * RET_DV, (h + 1) * RET_DV)
            qd = qd_ref[rows, kcols]
            kd = kd_ref[rows, kcols]
            v = v_ref[rows, vcols]
            scores = lax.dot_general(qd, kd, (((1,), (1,)), ((), ())), preferred_element_type=F32)
            p = jnp.where(causal, scores, 0.0).astype(BF16)
            state = s_ref[h]
            o = _dot(p, v) + _dot(qd, state.astype(BF16))
            kv = lax.dot_general(kd, v, (((0,), (0,)), ((), ())), preferred_element_type=F32)
            s_ref[h] = math.exp(chunk * LOG_GAMMA[h]) * (state + kv)
            on = o * lax.rsqrt(jnp.mean(o * o, axis=-1, keepdims=True) + EPS) * ng_ref[:, vcols]
            og_ref[rows, vcols] = (gate_ref[rows, vcols].astype(F32) * on).astype(BF16)

    o_ref[...] = x_ref[...] + _dot(og_ref[...], wout_ref[...])

    @pl.when(t == pl.num_programs(1) - 1)
    def _():
        sfin_ref[0] = s_ref[...]


def _ret(x_all, cos, sin, s0, gain, w_in, w_out, norm_gain, *, row0, n_streams, length, tile,
         chunk, shared_state):
    tiles = length // tile
    blk0 = row0 // tile
    state_shape = (1, RET_HEADS, RET_DK, RET_DV)
    s0_map = (lambda b, t: (0, 0, 0, 0)) if shared_state else (lambda b, t: (b, 0, 0, 0))
    return pl.pallas_call(
        functools.partial(_ret_body, tile=tile, chunk=chunk),
        grid=(n_streams, tiles),
        in_specs=[
            pl.BlockSpec((tile, D_MODEL), lambda b, t: (blk0 + b * tiles + t, 0)),
            pl.BlockSpec((tile, ROPE_HALF), lambda b, t: (t, 0)),
            pl.BlockSpec((tile, ROPE_HALF), lambda b, t: (t, 0)),
            pl.BlockSpec(state_shape, s0_map),
            pl.BlockSpec((1, D_MODEL), lambda b, t: (0, 0)),
            pl.BlockSpec(w_in.shape, lambda b, t: (0, 0), pipeline_mode=pl.Buffered(1)),
            pl.BlockSpec(w_out.shape, lambda b, t: (0, 0), pipeline_mode=pl.Buffered(1)),
            pl.BlockSpec((1, RET_VDIM), lambda b, t: (0, 0)),
        ],
        out_specs=[
            pl.BlockSpec((tile, D_MODEL), lambda b, t: (blk0 + b * tiles + t, 0)),
            pl.BlockSpec(state_shape, lambda b, t: (b, 0, 0, 0)),
        ],
        out_shape=[
            jax.ShapeDtypeStruct(x_all.shape, x_all.dtype),
            jax.ShapeDtypeStruct((n_streams,) + state_shape[1:], F32),
        ],
        scratch_shapes=[
            pltpu.VMEM((tile, D_MODEL), BF16),
            pltpu.VMEM((tile, D_MODEL), BF16),
            pltpu.VMEM((tile, RET_VDIM), BF16),
            pltpu.VMEM((tile, RET_VDIM), BF16),
            pltpu.VMEM((tile, RET_VDIM), BF16),
            pltpu.VMEM((RET_HEADS, RET_DK, RET_DV), F32),
            pltpu.VMEM((RET_HEADS, chunk, ROPE_HALF), F32),
            pltpu.VMEM((RET_HEADS, chunk, ROPE_HALF), F32),
        ],
        input_output_aliases={0: 0},
        compiler_params=pltpu.CompilerParams(
            dimension_semantics=("arbitrary", "arbitrary"), vmem_limit_bytes=VMEM_LIMIT_BYTES),
        name="ret",
    )(x_all, cos, sin, s0, gain, w_in, w_out, norm_gain)


def _norm_body(x_ref, g_ref, o_ref):
    o_ref[...] = _rms(x_ref[...], g_ref[...])


def _final_norm(x_all, gain, *, row0, rows, tile):
    blk0 = row0 // tile
    return pl.pallas_call(
        _norm_body,
        grid=(rows // tile,),
        in_specs=[
            pl.BlockSpec((tile, D_MODEL), lambda i: (blk0 + i, 0)),
            pl.BlockSpec((1, D_MODEL), lambda i: (0, 0)),
        ],
        out_specs=pl.BlockSpec((tile, D_MODEL), lambda i: (i, 0)),
        out_shape=jax.ShapeDtypeStruct((rows, D_MODEL), x_all.dtype),
        compiler_params=pltpu.CompilerParams(
            dimension_semantics=("arbitrary",), vmem_limit_bytes=VMEM_LIMIT_BYTES),
        name="final_norm",
    )(x_all, gain)


def _rope_tables(pos0, length):
    inv = ROPE_BASE ** (-jnp.arange(ROPE_HALF, dtype=jnp.float32) / ROPE_HALF)
    ang = (pos0 + jnp.arange(length)).astype(jnp.float32)[:, None] * inv[None, :]
    return jnp.cos(ang).astype(F32), jnp.sin(ang).astype(F32)


def kernel(x_prompt, x_sample, cache_pool, state_ret, meta_tokens, norm_g, final_norm_g, w_ffn1_in, w_ffn1_out, w_ffn2_in, w_ffn2_out, w_pool, pool_scale, w_ret_in, w_ret_out, ret_norm_g):
    batch, seq, _ = x_prompt.shape
    dec_batch, dec_seq, _ = x_sample.shape
    depth = norm_g.shape[0]
    assert meta_tokens.shape[0] == N_META_ROWS
    assert seq % max(POOL_TILE, RET_TILE) == 0

    prompt_row0 = 0
    sample_row0 = batch * seq
    meta_row0 = sample_row0 + dec_batch * dec_seq
    used_rows = meta_row0 + N_META_ROWS
    rows = pl.cdiv(used_rows, FFN_TILE) * FFN_TILE
    assert sample_row0 % dec_seq == 0 and meta_row0 % N_META_ROWS == 0
    x_all = jnp.concatenate([
        x_prompt.reshape(batch * seq, D_MODEL),
        x_sample.reshape(dec_batch * dec_seq, D_MODEL),
        meta_tokens.astype(x_prompt.dtype),
        jnp.zeros((rows - used_rows, D_MODEL), x_prompt.dtype)], axis=0)

    groups = (
        dict(name="meta", row0=meta_row0, n=1, length=N_META_ROWS, pos0=0),
        dict(name="prompt", row0=prompt_row0, n=batch, length=seq, pos0=N_META_ROWS),
        dict(name="sample", row0=sample_row0, n=dec_batch, length=dec_seq,
             pos0=N_META_ROWS + PAST_LEN),
    )
    rope = {g["name"]: _rope_tables(g["pos0"], g["length"]) for g in groups}
    zero_hist = jnp.zeros((1, HIST_ROWS, D_MODEL), F32)
    zero_state = jnp.zeros((1, RET_HEADS, RET_DK, RET_DV), F32)

    pool_p, pool_s, ret_p, ret_s = [], [], [], []
    for i in range(depth):
        gains = norm_g[i].astype(F32)
        x_all = _ffn(x_all, gains[0:1], w_ffn1_in[i].astype(BF16), w_ffn1_out[i].astype(BF16))
        j = i // 2
        if i % 2 == 0:
            wp = w_pool[j].astype(BF16)
            sc = pool_scale[j].astype(F32)[None, :]
            cache = jnp.pad(cache_pool[j].astype(F32), ((0, 0), (1, 0), (0, 0)))
            x_all, hist_meta = _pool(x_all, zero_hist, gains[1:2], wp, sc, row0=meta_row0,
                                     n_streams=1, length=N_META_ROWS, tile=N_META_ROWS,
                                     hist_valid=False, shared_hist=True)
            x_all, hist_p = _pool(x_all, hist_meta, gains[1:2], wp, sc, row0=prompt_row0,
                                  n_streams=batch, length=seq, tile=POOL_TILE,
                                  hist_valid=True, shared_hist=True)
            x_all, hist_s = _pool(x_all, cache, gains[1:2], wp, sc, row0=sample_row0,
                                  n_streams=dec_batch, length=dec_seq, tile=dec_seq,
                                  hist_valid=True, shared_hist=False)
            pool_p.append(hist_p[:, 1:])
            pool_s.append(hist_s[:, 1:])
        else:
            wi = w_ret_in[j].astype(BF16)
            wo = w_ret_out[j].astype(BF16)
            ng = ret_norm_g[j].astype(F32)[None, :]
            x_all, state_meta = _ret(x_all, *rope["meta"], zero_state, gains[1:2], wi, wo, ng,
                                     row0=meta_row0, n_streams=1, length=N_META_ROWS,
                                     tile=N_META_ROWS, chunk=N_META_ROWS, shared_state=True)
            x_all, state_p = _ret(x_all, *rope["prompt"], state_meta, gains[1:2], wi, wo, ng,
                                  row0=prompt_row0, n_streams=batch, length=seq,
                                  tile=RET_TILE, chunk=RET_CHUNK, shared_state=True)
            x_all, state_s = _ret(x_all, *rope["sample"], state_ret[j].astype(F32), gains[1:2],
                                  wi, wo, ng, row0=sample_row0, n_streams=dec_batch,
                                  length=dec_seq, tile=dec_seq, chunk=dec_seq, shared_state=False)
            ret_p.append(state_p)
            ret_s.append(state_s)
        x_all = _ffn(x_all, gains[2:3], w_ffn2_in[i].astype(BF16), w_ffn2_out[i].astype(BF16))

    fg = final_norm_g.astype(F32)[None, :]
    y_prompt = _final_norm(x_all, fg, row0=prompt_row0, rows=batch * seq, tile=FFN_TILE)
    y_sample = _final_norm(x_all, fg, row0=sample_row0, rows=dec_batch * dec_seq,
                           tile=dec_batch * dec_seq)
    return (y_prompt.reshape(batch, seq, D_MODEL),
            y_sample.reshape(dec_batch, dec_seq, D_MODEL),
            jnp.stack(pool_p), jnp.stack(pool_s), jnp.stack(ret_p), jnp.stack(ret_s))
```

```python
import functools
import math

import jax
import jax.numpy as jnp
from jax import lax
from jax.experimental import pallas as pl
from jax.experimental.pallas import tpu as pltpu

F32 = jnp.float32
BF16 = jnp.bfloat16

D_MODEL = 1024
D_FF = 2816
EPS = 1e-6
N_META_ROWS = 16
PAST_LEN = 1024

POOL_WINDOWS = (2, 4, 8, 16)
POOL_GROUP = D_MODEL // len(POOL_WINDOWS)
POOL_HIST = max(POOL_WINDOWS) - 1
HIST_ROWS = POOL_HIST + 1

RET_HEADS = 4
RET_DK = D_MODEL // RET_HEADS
RET_DV = 2 * D_MODEL // RET_HEADS
RET_VDIM = RET_HEADS * RET_DV
ROPE_BASE = 10000.0
ROPE_HALF = RET_DK // 2
LOG_GAMMA = tuple(math.log1p(-(2.0 ** (-5.0 - h))) for h in range(RET_HEADS))

VMEM_LIMIT_BYTES = 56 * 1024 * 1024

FFN_TILE = 1024
FFN_CHUNK = 256
RET_TILE = 512
RET_CHUNK = 256
PROJ_CHUNK = 256


def _rms(x, gain):
    return x * lax.rsqrt(jnp.mean(x * x, axis=-1, keepdims=True) + EPS) * gain


def _dot(a, b):
    return jnp.dot(a, b, preferred_element_type=F32)


def _params(n_grid_axes):
    return pltpu.CompilerParams(dimension_semantics=("arbitrary",) * n_grid_axes,
                                vmem_limit_bytes=VMEM_LIMIT_BYTES)


def _const_spec(shape, n_grid_axes, single_buffer=False):
    index_map = lambda *_: (0,) * len(shape)
    if single_buffer:
        return pl.BlockSpec(shape, index_map, pipeline_mode=pl.Buffered(1))
    return pl.BlockSpec(shape, index_map)


def _swiglu(x_ref, g_ref, win_ref, wout_ref, h_ref):
    u = _rms(x_ref[...], g_ref[...]).astype(BF16)
    for c in range(D_FF // FFN_CHUNK):
        lo = c * FFN_CHUNK
        a = _dot(u, win_ref[:, lo:lo + FFN_CHUNK])
        b = _dot(u, win_ref[:, D_FF + lo:D_FF + lo + FFN_CHUNK])
        h_ref[:, lo:lo + FFN_CHUNK] = (a * jax.nn.sigmoid(a) * b).astype(BF16)
    return x_ref[...] + 0.5 * _dot(h_ref[...], wout_ref[...])


def _pool_mix(x, hist_ref, g_ref, w_ref, scale_ref, o_ref, newhist_ref, ext_ref, *, hist_valid):
    tile = x.shape[0]
    t = pl.program_id(1)

    @pl.when(t == 0)
    def _():
        ext_ref[0:HIST_ROWS, :] = hist_ref[0]

    u = _rms(x, g_ref[...])
    ext_ref[HIST_ROWS:HIST_ROWS + tile, :] = u

    if not hist_valid:
        pos = t * tile + lax.broadcasted_iota(jnp.int32, (tile, 1), 0)

    for gi, w in enumerate(POOL_WINDOWS):
        cols = slice(gi * POOL_GROUP, (gi + 1) * POOL_GROUP)
        s = ext_ref[:, cols]
        k = 1
        while k < w:
            s = s + pltpu.roll(s, k, 0)
            k *= 2
        s = s[HIST_ROWS:, :]
        if hist_valid:
            mean = s * (1.0 / w)
        else:
            mean = s / jnp.minimum(pos + 1, w).astype(F32)
        pooled = (mean - u[:, cols]).astype(BF16)
        y = _dot(pooled, w_ref[gi]) * scale_ref[:, cols]
        o_ref[:, cols] = x[:, cols] + y

    last_rows = ext_ref[tile:tile + HIST_ROWS, :]
    ext_ref[0:HIST_ROWS, :] = last_rows

    @pl.when(t == pl.num_programs(1) - 1)
    def _():
        newhist_ref[0] = last_rows


def _ffn_body(x_ref, g_ref, win_ref, wout_ref, o_ref, h_ref):
    o_ref[...] = _swiglu(x_ref, g_ref, win_ref, wout_ref, h_ref)


def _ffn_norm_body(x_ref, g_ref, win_ref, wout_ref, fg_ref, o_ref, h_ref):
    o_ref[...] = _rms(_swiglu(x_ref, g_ref, win_ref, wout_ref, h_ref), fg_ref[...])


def _ffn_pool_body(x_ref, g_ref, win_ref, wout_ref, hist_ref, pg_ref, pw_ref, ps_ref,
                   o_ref, newhist_ref, h_ref, ext_ref):
    x1 = _swiglu(x_ref, g_ref, win_ref, wout_ref, h_ref)
    _pool_mix(x1, hist_ref, pg_ref, pw_ref, ps_ref, o_ref, newhist_ref, ext_ref, hist_valid=True)


def _pool_body(x_ref, hist_ref, pg_ref, pw_ref, ps_ref, o_ref, newhist_ref, ext_ref, *, hist_valid):
    _pool_mix(x_ref[...], hist_ref, pg_ref, pw_ref, ps_ref, o_ref, newhist_ref, ext_ref,
              hist_valid=hist_valid)


def _ffn(x, gain, w_in, w_out, *, tile, final_gain=None, in_place=False):
    rows = x.shape[0]
    row_spec = pl.BlockSpec((tile, D_MODEL), lambda i: (i, 0))
    in_specs = [row_spec, _const_spec((1, D_MODEL), 1),
                _const_spec(w_in.shape, 1, single_buffer=True),
                _const_spec(w_out.shape, 1, single_buffer=True)]
    args = [x, gain, w_in, w_out]
    body = _ffn_body
    if final_gain is not None:
        body = _ffn_norm_body
        in_specs.append(_const_spec((1, D_MODEL), 1))
        args.append(final_gain)
    return pl.pallas_call(
        body,
        grid=(rows // tile,),
        in_specs=in_specs,
        out_specs=row_spec,
        out_shape=jax.ShapeDtypeStruct(x.shape, x.dtype),
        scratch_shapes=[pltpu.VMEM((tile, D_FF), BF16)],
        input_output_aliases={0: 0} if in_place else {},
        compiler_params=_params(1),
        name="ffn",
    )(*args)


def _ffn_pool(x, gain, w_in, w_out, hist, pool_gain, w_pool, scale, *, n_streams, tile):
    tiles = x.shape[0] // n_streams // tile
    row_spec = pl.BlockSpec((tile, D_MODEL), lambda b, t: (b * tiles + t, 0))
    return pl.pallas_call(
        _ffn_pool_body,
        grid=(n_streams, tiles),
        in_specs=[row_spec, _const_spec((1, D_MODEL), 2),
                  _const_spec(w_in.shape, 2, single_buffer=True),
                  _const_spec(w_out.shape, 2, single_buffer=True),
                  _const_spec((1, HIST_ROWS, D_MODEL), 2),
                  _const_spec((1, D_MODEL), 2),
                  _const_spec(w_pool.shape, 2),
                  _const_spec((1, D_MODEL), 2)],
        out_specs=[row_spec, pl.BlockSpec((1, HIST_ROWS, D_MODEL), lambda b, t: (b, 0, 0))],
        out_shape=[jax.ShapeDtypeStruct(x.shape, x.dtype),
                   jax.ShapeDtypeStruct((n_streams, HIST_ROWS, D_MODEL), F32)],
        scratch_shapes=[pltpu.VMEM((tile, D_FF), BF16),
                        pltpu.VMEM((tile + HIST_ROWS, D_MODEL), F32)],
        compiler_params=_params(2),
        name="ffn_pool",
    )(x, gain, w_in, w_out, hist, pool_gain, w_pool, scale)


def _pool(x, hist, gain, w_pool, scale, *, row0, n_streams, length, hist_valid):
    blk0 = row0 // length
    row_spec = pl.BlockSpec((length, D_MODEL), lambda b, t: (blk0 + b, 0))
    return pl.pallas_call(
        functools.partial(_pool_body, hist_valid=hist_valid),
        grid=(n_streams, 1),
        in_specs=[row_spec,
                  pl.BlockSpec((1, HIST_ROWS, D_MODEL), lambda b, t: (b, 0, 0)),
                  _const_spec((1, D_MODEL), 2),
                  _const_spec(w_pool.shape, 2),
                  _const_spec((1, D_MODEL), 2)],
        out_specs=[row_spec, pl.BlockSpec((1, HIST_ROWS, D_MODEL), lambda b, t: (b, 0, 0))],
        out_shape=[jax.ShapeDtypeStruct(x.shape, x.dtype),
                   jax.ShapeDtypeStruct((n_streams, HIST_ROWS, D_MODEL), F32)],
        scratch_shapes=[pltpu.VMEM((length + HIST_ROWS, D_MODEL), F32)],
        input_output_aliases={0: 0},
        compiler_params=_params(2),
        name="pool",
    )(x, hist, gain, w_pool, scale)


def _ret_body(x_ref, cos_ref, sin_ref, s0_ref, g_ref, win_ref, wout_ref, ng_ref,
              o_ref, sfin_ref,
              qd_ref, kd_ref, v_ref, gate_ref, og_ref, s_ref, dq_ref, dk_ref,
              *, tile, chunk):
    b = pl.program_id(0)
    t = pl.program_id(1)
    n_chunks = tile // chunk

    @pl.when((b == 0) & (t == 0))
    def _():
        steps = (lax.broadcasted_iota(jnp.int32, (chunk, ROPE_HALF), 0) + 1).astype(F32)
        for h in range(RET_HEADS):
            dq_ref[h] = jnp.exp(steps * LOG_GAMMA[h])
            dk_ref[h] = jnp.exp(steps * (-LOG_GAMMA[h])) * (RET_DK ** -0.5)

    @pl.when(t == 0)
    def _():
        s_ref[...] = s0_ref[0]

    u = _rms(x_ref[...], g_ref[...]).astype(BF16)
    cos = cos_ref[...]
    sin = sin_ref[...]

    def rotate(p):
        x1 = p[:, :ROPE_HALF]
        x2 = p[:, ROPE_HALF:]
        return x1 * cos - x2 * sin, x1 * sin + x2 * cos

    for h in range(RET_HEADS):
        lo = h * RET_DK
        for off, dec_ref, dst_ref in ((0, dq_ref, qd_ref), (D_MODEL, dk_ref, kd_ref)):
            r1, r2 = rotate(_dot(u, win_ref[:, off + lo:off + lo + RET_DK]))
            dec = dec_ref[h]
            for ci in range(n_chunks):
                rows = slice(ci * chunk, (ci + 1) * chunk)
                dst_ref[rows, lo:lo + ROPE_HALF] = (r1[rows] * dec).astype(BF16)
                dst_ref[rows, lo + ROPE_HALF:lo + RET_DK] = (r2[rows] * dec).astype(BF16)

    for c in range(RET_VDIM // PROJ_CHUNK):
        lo = c * PROJ_CHUNK
        v_ref[:, lo:lo + PROJ_CHUNK] = _dot(
            u, win_ref[:, 2 * D_MODEL + lo:2 * D_MODEL + lo + PROJ_CHUNK]).astype(BF16)
        gp = _dot(u, win_ref[:, 2 * D_MODEL + RET_VDIM + lo:2 * D_MODEL + RET_VDIM + lo + PROJ_CHUNK])
        gate_ref[:, lo:lo + PROJ_CHUNK] = (gp * jax.nn.sigmoid(gp)).astype(BF16)

    causal = (lax.broadcasted_iota(jnp.int32, (chunk, chunk), 0)
              >= lax.broadcasted_iota(jnp.int32, (chunk, chunk), 1))
    for ci in range(n_chunks):
        rows = slice(ci * chunk, (ci + 1) * chunk)
        for h in range(RET_HEADS):
            kcols = slice(h * RET_DK, (h + 1) * RET_DK)
            vcols = slice(h * RET_DV, (h + 1) * RET_DV)
            qd = qd_ref[rows, kcols]
            kd = kd_ref[rows, kcols]
            v = v_ref[rows, vcols]
            scores = lax.dot_general(qd, kd, (((1,), (1,)), ((), ())), preferred_element_type=F32)
            p = jnp.where(causal, scores, 0.0).astype(BF16)
            state = s_ref[h]
            o = _dot(p, v) + _dot(qd, state.astype(BF16))
            kv = lax.dot_general(kd, v, (((0,), (0,)), ((), ())), preferred_element_type=F32)
            s_ref[h] = math.exp(chunk * LOG_GAMMA[h]) * (state + kv)
            on = o * lax.rsqrt(jnp.mean(o * o, axis=-1, keepdims=True) + EPS) * ng_ref[:, vcols]
            og_ref[rows, vcols] = (gate_ref[rows, vcols].astype(F32) * on).astype(BF16)

    o_ref[...] = x_ref[...] + _dot(og_ref[...], wout_ref[...])

    @pl.when(t == pl.num_programs(1) - 1)
    def _():
        sfin_ref[0] = s_ref[...]


def _ret(x, cos, sin, s0, gain, w_in, w_out, norm_gain, *, row0, n_streams, length, tile,
         chunk, shared_state, in_place):
    tiles = length // tile
    blk0 = row0 // tile
    state_shape = (1, RET_HEADS, RET_DK, RET_DV)
    s0_map = (lambda b, t: (0, 0, 0, 0)) if shared_state else (lambda b, t: (b, 0, 0, 0))
    row_spec = pl.BlockSpec((tile, D_MODEL), lambda b, t: (blk0 + b * tiles + t, 0))
    rope_spec = pl.BlockSpec((tile, ROPE_HALF), lambda b, t: (t, 0))
    return pl.pallas_call(
        functools.partial(_ret_body, tile=tile, chunk=chunk),
        grid=(n_streams, tiles),
        in_specs=[row_spec, rope_spec, rope_spec,
                  pl.BlockSpec(state_shape, s0_map),
                  _const_spec((1, D_MODEL), 2),
                  _const_spec(w_in.shape, 2, single_buffer=True),
                  _const_spec(w_out.shape, 2, single_buffer=True),
                  _const_spec((1, RET_VDIM), 2)],
        out_specs=[row_spec, pl.BlockSpec(state_shape, lambda b, t: (b, 0, 0, 0))],
        out_shape=[jax.ShapeDtypeStruct(x.shape, x.dtype),
                   jax.ShapeDtypeStruct((n_streams,) + state_shape[1:], F32)],
        scratch_shapes=[
            pltpu.VMEM((tile, D_MODEL), BF16),
            pltpu.VMEM((tile, D_MODEL), BF16),
            pltpu.VMEM((tile, RET_VDIM), BF16),
            pltpu.VMEM((tile, RET_VDIM), BF16),
            pltpu.VMEM((tile, RET_VDIM), BF16),
            pltpu.VMEM((RET_HEADS, RET_DK, RET_DV), F32),
            pltpu.VMEM((RET_HEADS, chunk, ROPE_HALF), F32),
            pltpu.VMEM((RET_HEADS, chunk, ROPE_HALF), F32),
        ],
        input_output_aliases={0: 0} if in_place else {},
        compiler_params=_params(2),
        name="ret",
    )(x, cos, sin, s0, gain, w_in, w_out, norm_gain)


def _rope_tables(pos0, length):
    inv = ROPE_BASE ** (-jnp.arange(ROPE_HALF, dtype=jnp.float32) / ROPE_HALF)
    ang = (pos0 + jnp.arange(length)).astype(jnp.float32)[:, None] * inv[None, :]
    return jnp.cos(ang).astype(F32), jnp.sin(ang).astype(F32)


def kernel(x_prompt, x_sample, cache_pool, state_ret, meta_tokens, norm_g, final_norm_g, w_ffn1_in, w_ffn1_out, w_ffn2_in, w_ffn2_out, w_pool, pool_scale, w_ret_in, w_ret_out, ret_norm_g):
    batch, seq, _ = x_prompt.shape
    dec_batch, dec_seq, _ = x_sample.shape
    depth = norm_g.shape[0]
    assert meta_tokens.shape[0] == N_META_ROWS
    assert seq % FFN_TILE == 0 and seq % RET_TILE == 0

    meta_row0 = dec_batch * dec_seq
    small_rows = meta_row0 + N_META_ROWS
    assert meta_row0 % N_META_ROWS == 0
    xs = jnp.concatenate([x_sample.reshape(meta_row0, D_MODEL),
                          meta_tokens.astype(x_sample.dtype)], axis=0)
    xp = x_prompt.reshape(batch * seq, D_MODEL)

    rope_meta = _rope_tables(0, N_META_ROWS)
    rope_prompt = _rope_tables(N_META_ROWS, seq)
    rope_sample = _rope_tables(N_META_ROWS + PAST_LEN, dec_seq)
    zero_hist = jnp.zeros((1, HIST_ROWS, D_MODEL), F32)
    zero_state = jnp.zeros((1, RET_HEADS, RET_DK, RET_DV), F32)
    final_gain = final_norm_g.astype(F32)[None, :]

    pool_p, pool_s, ret_p, ret_s = [], [], [], []
    for i in range(depth):
        gains = norm_g[i].astype(F32)
        w1 = (gains[0:1], w_ffn1_in[i].astype(BF16), w_ffn1_out[i].astype(BF16))
        w2 = (gains[2:3], w_ffn2_in[i].astype(BF16), w_ffn2_out[i].astype(BF16))
        last = i == depth - 1
        j = i // 2
        xs = _ffn(xs, *w1, tile=small_rows, in_place=True)
        if i % 2 == 0:
            wp = w_pool[j].astype(BF16)
            sc = pool_scale[j].astype(F32)[None, :]
            cache = jnp.pad(cache_pool[j].astype(F32), ((0, 0), (1, 0), (0, 0)))
            xs, hist_meta = _pool(xs, zero_hist, gains[1:2], wp, sc, row0=meta_row0, n_streams=1,
                                  length=N_META_ROWS, hist_valid=False)
            xs, hist_s = _pool(xs, cache, gains[1:2], wp, sc, row0=0, n_streams=dec_batch,
                               length=dec_seq, hist_valid=True)
            xp, hist_p = _ffn_pool(xp, *w1, hist_meta, gains[1:2], wp, sc, n_streams=batch,
                                   tile=FFN_TILE)
            pool_p.append(hist_p[:, 1:])
            pool_s.append(hist_s[:, 1:])
        else:
            wi = w_ret_in[j].astype(BF16)
            wo = w_ret_out[j].astype(BF16)
            ng = ret_norm_g[j].astype(F32)[None, :]
            xs, state_meta = _ret(xs, *rope_meta, zero_state, gains[1:2], wi, wo, ng,
                                  row0=meta_row0, n_streams=1, length=N_META_ROWS,
                                  tile=N_META_ROWS, chunk=N_META_ROWS, shared_state=True,
                                  in_place=True)
            xs, state_s = _ret(xs, *rope_sample, state_ret[j].astype(F32), gains[1:2], wi, wo, ng,
                               row0=0, n_streams=dec_batch, length=dec_seq, tile=dec_seq,
                               chunk=dec_seq, shared_state=False, in_place=True)
            xp = _ffn(xp, *w1, tile=FFN_TILE)
            xp, state_p = _ret(xp, *rope_prompt, state_meta, gains[1:2], wi, wo, ng,
                               row0=0, n_streams=batch, length=seq, tile=RET_TILE,
                               chunk=RET_CHUNK, shared_state=True, in_place=False)
            ret_p.append(state_p)
            ret_s.append(state_s)
        xs = _ffn(xs, *w2, tile=small_rows, final_gain=final_gain if last else None,
                  in_place=True)
        xp = _ffn(xp, *w2, tile=FFN_TILE, final_gain=final_gain if last else None)

    return (xp.reshape(batch, seq, D_MODEL),
            xs[:meta_row0].reshape(dec_batch, dec_seq, D_MODEL),
            jnp.stack(pool_p), jnp.stack(pool_s), jnp.stack(ret_p), jnp.stack(ret_s))
```

```python
import functools
import math

import jax
import jax.numpy as jnp
from jax import lax
from jax.experimental import pallas as pl
from jax.experimental.pallas import tpu as pltpu

F32 = jnp.float32
BF16 = jnp.bfloat16

D_MODEL = 1024
D_FF = 2816
EPS = 1e-6
N_META_ROWS = 16
PAST_LEN = 1024

POOL_WINDOWS = (2, 4, 8, 16)
POOL_GROUP = D_MODEL // len(POOL_WINDOWS)
POOL_HIST = max(POOL_WINDOWS) - 1
HIST_ROWS = POOL_HIST + 1

RET_HEADS = 4
RET_DK = D_MODEL // RET_HEADS
RET_DV = 2 * D_MODEL // RET_HEADS
RET_VDIM = RET_HEADS * RET_DV
ROPE_BASE = 10000.0
ROPE_HALF = RET_DK // 2
LOG_GAMMA = tuple(math.log1p(-(2.0 ** (-5.0 - h))) for h in range(RET_HEADS))

VMEM_LIMIT_BYTES = 56 * 1024 * 1024

FFN_TILE = 1024
FFN_POOL_TILE = 512
FFN_CHUNK = 256
RET_TILE = 512
RET_CHUNK = 256
PROJ_CHUNK = 256
POOL_PROJECT_AFTER_CHUNK = (3, 5, 7, 9)


def _rms(x, gain):
    return x * lax.rsqrt(jnp.mean(x * x, axis=-1, keepdims=True) + EPS) * gain


def _dot(a, b):
    return jnp.dot(a, b, preferred_element_type=F32)


def _params(n_grid_axes):
    return pltpu.CompilerParams(dimension_semantics=("arbitrary",) * n_grid_axes,
                                vmem_limit_bytes=VMEM_LIMIT_BYTES)


def _const_spec(shape, n_grid_axes, single_buffer=False):
    index_map = lambda *_: (0,) * len(shape)
    if single_buffer:
        return pl.BlockSpec(shape, index_map, pipeline_mode=pl.Buffered(1))
    return pl.BlockSpec(shape, index_map)


def _swiglu(x_ref, g_ref, win_ref, wout_ref, h_ref, after_chunk=None):
    u = _rms(x_ref[...], g_ref[...]).astype(BF16)
    for c in range(D_FF // FFN_CHUNK):
        lo = c * FFN_CHUNK
        a = _dot(u, win_ref[:, lo:lo + FFN_CHUNK])
        b = _dot(u, win_ref[:, D_FF + lo:D_FF + lo + FFN_CHUNK])
        h_ref[:, lo:lo + FFN_CHUNK] = (a * jax.nn.sigmoid(a) * b).astype(BF16)
        if after_chunk is not None:
            after_chunk(c)
    return x_ref[...] + 0.5 * _dot(h_ref[...], wout_ref[...])


def _pool_windows(x, t, hist_ref, g_ref, newhist_ref, ext_ref, pooled_ref, *, hist_valid):
    tile = x.shape[0]
    ext_ref[0:HIST_ROWS, :] = jnp.where(t == 0, hist_ref[0], ext_ref[0:HIST_ROWS, :])
    u = _rms(x, g_ref[...])
    ext_ref[HIST_ROWS:HIST_ROWS + tile, :] = u

    if not hist_valid:
        pos = t * tile + lax.broadcasted_iota(jnp.int32, (tile, 1), 0)

    for gi, w in enumerate(POOL_WINDOWS):
        cols = slice(gi * POOL_GROUP, (gi + 1) * POOL_GROUP)
        s = ext_ref[:, cols]
        k = 1
        while k < w:
            s = s + pltpu.roll(s, k, 0)
            k *= 2
        s = s[HIST_ROWS:, :]
        if hist_valid:
            mean = s * (1.0 / w)
        else:
            mean = s / jnp.minimum(pos + 1, w).astype(F32)
        pooled_ref[:, cols] = (mean - u[:, cols]).astype(BF16)

    last_rows = ext_ref[tile:tile + HIST_ROWS, :]
    ext_ref[0:HIST_ROWS, :] = last_rows
    newhist_ref[0] = last_rows


def _pool_project(gi, pooled_ref, w_ref, scale_ref, o_ref):
    cols = slice(gi * POOL_GROUP, (gi + 1) * POOL_GROUP)
    o_ref[:, cols] += _dot(pooled_ref[:, cols], w_ref[gi]) * scale_ref[:, cols]


def _ffn_body(x_ref, g_ref, win_ref, wout_ref, o_ref, h_ref):
    o_ref[...] = _swiglu(x_ref, g_ref, win_ref, wout_ref, h_ref)


def _ffn_norm_body(x_ref, g_ref, win_ref, wout_ref, fg_ref, o_ref, h_ref):
    o_ref[...] = _rms(_swiglu(x_ref, g_ref, win_ref, wout_ref, h_ref), fg_ref[...])


def _ffn_pool_body(x_ref, g_ref, win_ref, wout_ref, hist_ref, pg_ref, pw_ref, ps_ref,
                   o_ref, newhist_ref, h_ref, x1_ref, pooled_ref, ext_ref, *, tiles_per_stream):
    i = pl.program_id(0)

    @pl.when(i == 0)
    def _():
        x1_ref[...] = jnp.zeros_like(x1_ref)
        ext_ref[...] = jnp.zeros_like(ext_ref)

    t = lax.rem(i - 1, jnp.int32(tiles_per_stream))
    x1 = x1_ref[...]
    o_ref[...] = x1
    _pool_windows(x1, t, hist_ref, pg_ref, newhist_ref, ext_ref, pooled_ref, hist_valid=True)

    def project_behind(c):
        if c in POOL_PROJECT_AFTER_CHUNK:
            _pool_project(POOL_PROJECT_AFTER_CHUNK.index(c), pooled_ref, pw_ref, ps_ref, o_ref)

    x1_ref[...] = _swiglu(x_ref, g_ref, win_ref, wout_ref, h_ref, project_behind)


def _pool_body(x_ref, hist_ref, pg_ref, pw_ref, ps_ref, o_ref, newhist_ref, ext_ref, pooled_ref,
               *, hist_valid):
    x = x_ref[...]
    o_ref[...] = x
    _pool_windows(x, 0, hist_ref, pg_ref, newhist_ref, ext_ref, pooled_ref, hist_valid=hist_valid)
    for gi in range(len(POOL_WINDOWS)):
        _pool_project(gi, pooled_ref, pw_ref, ps_ref, o_ref)


def _ffn(x, gain, w_in, w_out, *, tile, final_gain=None, in_place=False):
    row_spec = pl.BlockSpec((tile, D_MODEL), lambda i: (i, 0))
    in_specs = [row_spec, _const_spec((1, D_MODEL), 1),
                _const_spec(w_in.shape, 1, single_buffer=True),
                _const_spec(w_out.shape, 1, single_buffer=True)]
    args = [x, gain, w_in, w_out]
    body = _ffn_body
    if final_gain is not None:
        body = _ffn_norm_body
        in_specs.append(_const_spec((1, D_MODEL), 1))
        args.append(final_gain)
    return pl.pallas_call(
        body,
        grid=(x.shape[0] // tile,),
        in_specs=in_specs,
        out_specs=row_spec,
        out_shape=jax.ShapeDtypeStruct(x.shape, x.dtype),
        scratch_shapes=[pltpu.VMEM((tile, D_FF), BF16)],
        input_output_aliases={0: 0} if in_place else {},
        compiler_params=_params(1),
        name="ffn",
    )(*args)


def _ffn_pool(x, gain, w_in, w_out, hist, pool_gain, w_pool, scale, *, n_streams, tile):
    n = x.shape[0] // tile
    tiles_per_stream = n // n_streams
    row_spec = pl.BlockSpec((tile, D_MODEL), lambda i: (jnp.minimum(i, n - 1), 0))
    out_spec = pl.BlockSpec((tile, D_MODEL), lambda i: (jnp.maximum(i - 1, 0), 0))
    newhist_spec = pl.BlockSpec((1, HIST_ROWS, D_MODEL),
                                lambda i: (jnp.maximum(i - 1, 0) // tiles_per_stream, 0, 0))
    return pl.pallas_call(
        functools.partial(_ffn_pool_body, tiles_per_stream=tiles_per_stream),
        grid=(n + 1,),
        in_specs=[row_spec, _const_spec((1, D_MODEL), 1),
                  _const_spec(w_in.shape, 1, single_buffer=True),
                  _const_spec(w_out.shape, 1, single_buffer=True),
                  _const_spec((1, HIST_ROWS, D_MODEL), 1),
                  _const_spec((1, D_MODEL), 1),
                  _const_spec(w_pool.shape, 1),
                  _const_spec((1, D_MODEL), 1)],
        out_specs=[out_spec, newhist_spec],
        out_shape=[jax.ShapeDtypeStruct(x.shape, x.dtype),
                   jax.ShapeDtypeStruct((n_streams, HIST_ROWS, D_MODEL), F32)],
        scratch_shapes=[pltpu.VMEM((tile, D_FF), BF16),
                        pltpu.VMEM((tile, D_MODEL), F32),
                        pltpu.VMEM((tile, D_MODEL), BF16),
                        pltpu.VMEM((tile + HIST_ROWS, D_MODEL), F32)],
        compiler_params=_params(1),
        name="ffn_pool",
    )(x, gain, w_in, w_out, hist, pool_gain, w_pool, scale)


def _pool(x, hist, gain, w_pool, scale, *, row0, n_streams, length, hist_valid):
    blk0 = row0 // length
    row_spec = pl.BlockSpec((length, D_MODEL), lambda b: (blk0 + b, 0))
    hist_spec = pl.BlockSpec((1, HIST_ROWS, D_MODEL), lambda b: (b, 0, 0))
    return pl.pallas_call(
        functools.partial(_pool_body, hist_valid=hist_valid),
        grid=(n_streams,),
        in_specs=[row_spec, hist_spec,
                  _const_spec((1, D_MODEL), 1),
                  _const_spec(w_pool.shape, 1),
                  _const_spec((1, D_MODEL), 1)],
        out_specs=[row_spec, hist_spec],
        out_shape=[jax.ShapeDtypeStruct(x.shape, x.dtype),
                   jax.ShapeDtypeStruct((n_streams, HIST_ROWS, D_MODEL), F32)],
        scratch_shapes=[pltpu.VMEM((length + HIST_ROWS, D_MODEL), F32),
                        pltpu.VMEM((length, D_MODEL), BF16)],
        input_output_aliases={0: 0},
        compiler_params=_params(1),
        name="pool",
    )(x, hist, gain, w_pool, scale)


def _ret_body(x_ref, cos_ref, sin_ref, s0_ref, g_ref, win_ref, wout_ref, ng_ref,
              o_ref, sfin_ref,
              qd_ref, kd_ref, v_ref, gate_ref, og_ref, s_ref, dq_ref, dk_ref,
              *, tile, chunk):
    b = pl.program_id(0)
    t = pl.program_id(1)
    n_chunks = tile // chunk

    @pl.when((b == 0) & (t == 0))
    def _():
        steps = (lax.broadcasted_iota(jnp.int32, (chunk, ROPE_HALF), 0) + 1).astype(F32)
        for h in range(RET_HEADS):
            dq_ref[h] = jnp.exp(steps * LOG_GAMMA[h])
            dk_ref[h] = jnp.exp(steps * (-LOG_GAMMA[h])) * (RET_DK ** -0.5)

    @pl.when(t == 0)
    def _():
        s_ref[...] = s0_ref[0]

    u = _rms(x_ref[...], g_ref[...]).astype(BF16)
    cos = cos_ref[...]
    sin = sin_ref[...]

    def rotate(p):
        x1 = p[:, :ROPE_HALF]
        x2 = p[:, ROPE_HALF:]
        return x1 * cos - x2 * sin, x1 * sin + x2 * cos

    for h in range(RET_HEADS):
        lo = h * RET_DK
        for off, dec_ref, dst_ref in ((0, dq_ref, qd_ref), (D_MODEL, dk_ref, kd_ref)):
            r1, r2 = rotate(_dot(u, win_ref[:, off + lo:off + lo + RET_DK]))
            dec = dec_ref[h]
            for ci in range(n_chunks):
                rows = slice(ci * chunk, (ci + 1) * chunk)
                dst_ref[rows, lo:lo + ROPE_HALF] = (r1[rows] * dec).astype(BF16)
                dst_ref[rows, lo + ROPE_HALF:lo + RET_DK] = (r2[rows] * dec).astype(BF16)

    for c in range(RET_VDIM // PROJ_CHUNK):
        lo = c * PROJ_CHUNK
        v_ref[:, lo:lo + PROJ_CHUNK] = _dot(
            u, win_ref[:, 2 * D_MODEL + lo:2 * D_MODEL + lo + PROJ_CHUNK]).astype(BF16)
        gp = _dot(u, win_ref[:, 2 * D_MODEL + RET_VDIM + lo:2 * D_MODEL + RET_VDIM + lo + PROJ_CHUNK])
        gate_ref[:, lo:lo + PROJ_CHUNK] = (gp * jax.nn.sigmoid(gp)).astype(BF16)

    causal = (lax.broadcasted_iota(jnp.int32, (chunk, chunk), 0)
              >= lax.broadcasted_iota(jnp.int32, (chunk, chunk), 1))
    for ci in range(n_chunks):
        rows = slice(ci * chunk, (ci + 1) * chunk)
        for h in range(RET_HEADS):
            kcols = slice(h * RET_DK, (h + 1) * RET_DK)
            vcols = slice(h * RET_DV, (h + 1) * RET_DV)
            qd = qd_ref[rows, kcols]
            kd = kd_ref[rows, kcols]
            v = v_ref[rows, vcols]
            scores = lax.dot_general(qd, kd, (((1,), (1,)), ((), ())), preferred_element_type=F32)
            p = jnp.where(causal, scores, 0.0).astype(BF16)
            state = s_ref[h]
            o = _dot(p, v) + _dot(qd, state.astype(BF16))
            kv = lax.dot_general(kd, v, (((0,), (0,)), ((), ())), preferred_element_type=F32)
            s_ref[h] = math.exp(chunk * LOG_GAMMA[h]) * (state + kv)
            on = o * lax.rsqrt(jnp.mean(o * o, axis=-1, keepdims=True) + EPS) * ng_ref[:, vcols]
            og_ref[rows, vcols] = (gate_ref[rows, vcols].astype(F32) * on).astype(BF16)

    o_ref[...] = x_ref[...] + _dot(og_ref[...], wout_ref[...])

    @pl.when(t == pl.num_programs(1) - 1)
    def _():
        sfin_ref[0] = s_ref[...]


def _ret(x, cos, sin, s0, gain, w_in, w_out, norm_gain, *, row0, n_streams, length, tile,
         chunk, shared_state, in_place):
    tiles = length // tile
    blk0 = row0 // tile
    state_shape = (1, RET_HEADS, RET_DK, RET_DV)
    s0_map = (lambda b, t: (0, 0, 0, 0)) if shared_state else (lambda b, t: (b, 0, 0, 0))
    row_spec = pl.BlockSpec((tile, D_MODEL), lambda b, t: (blk0 + b * tiles + t, 0))
    rope_spec = pl.BlockSpec((tile, ROPE_HALF), lambda b, t: (t, 0))
    return pl.pallas_call(
        functools.partial(_ret_body, tile=tile, chunk=chunk),
        grid=(n_streams, tiles),
        in_specs=[row_spec, rope_spec, rope_spec,
                  pl.BlockSpec(state_shape, s0_map),
                  _const_spec((1, D_MODEL), 2),
                  _const_spec(w_in.shape, 2, single_buffer=True),
                  _const_spec(w_out.shape, 2, single_buffer=True),
                  _const_spec((1, RET_VDIM), 2)],
        out_specs=[row_spec, pl.BlockSpec(state_shape, lambda b, t: (b, 0, 0, 0))],
        out_shape=[jax.ShapeDtypeStruct(x.shape, x.dtype),
                   jax.ShapeDtypeStruct((n_streams,) + state_shape[1:], F32)],
        scratch_shapes=[
            pltpu.VMEM((tile, D_MODEL), BF16),
            pltpu.VMEM((tile, D_MODEL), BF16),
            pltpu.VMEM((tile, RET_VDIM), BF16),
            pltpu.VMEM((tile, RET_VDIM), BF16),
            pltpu.VMEM((tile, RET_VDIM), BF16),
            pltpu.VMEM((RET_HEADS, RET_DK, RET_DV), F32),
            pltpu.VMEM((RET_HEADS, chunk, ROPE_HALF), F32),
            pltpu.VMEM((RET_HEADS, chunk, ROPE_HALF), F32),
        ],
        input_output_aliases={0: 0} if in_place else {},
        compiler_params=_params(2),
        name="ret",
    )(x, cos, sin, s0, gain, w_in, w_out, norm_gain)


def _rope_tables(pos0, length):
    inv = ROPE_BASE ** (-jnp.arange(ROPE_HALF, dtype=jnp.float32) / ROPE_HALF)
    ang = (pos0 + jnp.arange(length)).astype(jnp.float32)[:, None] * inv[None, :]
    return jnp.cos(ang).astype(F32), jnp.sin(ang).astype(F32)


def kernel(x_prompt, x_sample, cache_pool, state_ret, meta_tokens, norm_g, final_norm_g, w_ffn1_in, w_ffn1_out, w_ffn2_in, w_ffn2_out, w_pool, pool_scale, w_ret_in, w_ret_out, ret_norm_g):
    batch, seq, _ = x_prompt.shape
    dec_batch, dec_seq, _ = x_sample.shape
    depth = norm_g.shape[0]
    assert meta_tokens.shape[0] == N_META_ROWS
    assert seq % FFN_TILE == 0 and seq % FFN_POOL_TILE == 0 and seq % RET_TILE == 0

    meta_row0 = dec_batch * dec_seq
    small_rows = meta_row0 + N_META_ROWS
    assert meta_row0 % N_META_ROWS == 0
    xs = jnp.concatenate([x_sample.reshape(meta_row0, D_MODEL),
                          meta_tokens.astype(x_sample.dtype)], axis=0)
    xp = x_prompt.reshape(batch * seq, D_MODEL)

    rope_meta = _rope_tables(0, N_META_ROWS)
    rope_prompt = _rope_tables(N_META_ROWS, seq)
    rope_sample = _rope_tables(N_META_ROWS + PAST_LEN, dec_seq)
    zero_hist = jnp.zeros((1, HIST_ROWS, D_MODEL), F32)
    zero_state = jnp.zeros((1, RET_HEADS, RET_DK, RET_DV), F32)
    final_gain = final_norm_g.astype(F32)[None, :]

    pool_p, pool_s, ret_p, ret_s = [], [], [], []
    for i in range(depth):
        gains = norm_g[i].astype(F32)
        w1 = (gains[0:1], w_ffn1_in[i].astype(BF16), w_ffn1_out[i].astype(BF16))
        w2 = (gains[2:3], w_ffn2_in[i].astype(BF16), w_ffn2_out[i].astype(BF16))
        last = i == depth - 1
        j = i // 2
        xs = _ffn(xs, *w1, tile=small_rows, in_place=True)
        if i % 2 == 0:
            wp = w_pool[j].astype(BF16)
            sc = pool_scale[j].astype(F32)[None, :]
            cache = jnp.pad(cache_pool[j].astype(F32), ((0, 0), (1, 0), (0, 0)))
            xs, hist_meta = _pool(xs, zero_hist, gains[1:2], wp, sc, row0=meta_row0, n_streams=1,
                                  length=N_META_ROWS, hist_valid=False)
            xs, hist_s = _pool(xs, cache, gains[1:2], wp, sc, row0=0, n_streams=dec_batch,
                               length=dec_seq, hist_valid=True)
            xp, hist_p = _ffn_pool(xp, *w1, hist_meta, gains[1:2], wp, sc, n_streams=batch,
                                   tile=FFN_POOL_TILE)
            pool_p.append(hist_p[:, 1:])
            pool_s.append(hist_s[:, 1:])
        else:
            wi = w_ret_in[j].astype(BF16)
            wo = w_ret_out[j].astype(BF16)
            ng = ret_norm_g[j].astype(F32)[None, :]
            xs, state_meta = _ret(xs, *rope_meta, zero_state, gains[1:2], wi, wo, ng,
                                  row0=meta_row0, n_streams=1, length=N_META_ROWS,
                                  tile=N_META_ROWS, chunk=N_META_ROWS, shared_state=True,
                                  in_place=True)
            xs, state_s = _ret(xs, *rope_sample, state_ret[j].astype(F32), gains[1:2], wi, wo, ng,
                               row0=0, n_streams=dec_batch, length=dec_seq, tile=dec_seq,
                               chunk=dec_seq, shared_state=False, in_place=True)
            xp = _ffn(xp, *w1, tile=FFN_TILE)
            xp, state_p = _ret(xp, *rope_prompt, state_meta, gains[1:2], wi, wo, ng,
                               row0=0, n_streams=batch, length=seq, tile=RET_TILE,
                               chunk=RET_CHUNK, shared_state=True, in_place=False)
            ret_p.append(state_p)
            ret_s.append(state_s)
        xs = _ffn(xs, *w2, tile=small_rows, final_gain=final_gain if last else None,
                  in_place=True)
        xp = _ffn(xp, *w2, tile=FFN_TILE, final_gain=final_gain if last else None)

    return (xp.reshape(batch, seq, D_MODEL),
            xs[:meta_row0].reshape(dec_batch, dec_seq, D_MODEL),
            jnp.stack(pool_p), jnp.stack(pool_s), jnp.stack(ret_p), jnp.stack(ret_s))
```

```python
import functools
import math

import jax
import jax.numpy as jnp
from jax import lax
from jax.experimental import pallas as pl
from jax.experimental.pallas import tpu as pltpu

F32 = jnp.float32
BF16 = jnp.bfloat16

D_MODEL = 1024
D_FF = 2816
EPS = 1e-6
N_META_ROWS = 16
PAST_LEN = 1024

POOL_WINDOWS = (2, 4, 8, 16)
POOL_GROUP = D_MODEL // len(POOL_WINDOWS)
POOL_HIST = max(POOL_WINDOWS) - 1
HIST_ROWS = POOL_HIST + 1

RET_HEADS = 4
RET_DK = D_MODEL // RET_HEADS
RET_DV = 2 * D_MODEL // RET_HEADS
RET_VDIM = RET_HEADS * RET_DV
ROPE_BASE = 10000.0
ROPE_HALF = RET_DK // 2
LOG_GAMMA = tuple(math.log1p(-(2.0 ** (-5.0 - h))) for h in range(RET_HEADS))

VMEM_LIMIT_BYTES = 56 * 1024 * 1024

FFN_TILE = 1024
FFN_POOL_TILE = 512
FFN_CHUNK = 256
RET_TILE = 512
RET_CHUNK = 256
PROJ_CHUNK = 256
POOL_PROJECT_AFTER_CHUNK = (3, 5, 7, 9)


def _rms(x, gain):
    return x * lax.rsqrt(jnp.mean(x * x, axis=-1, keepdims=True) + EPS) * gain


def _dot(a, b):
    return jnp.dot(a, b, preferred_element_type=F32)


def _params(n_grid_axes):
    return pltpu.CompilerParams(dimension_semantics=("arbitrary",) * n_grid_axes,
                                vmem_limit_bytes=VMEM_LIMIT_BYTES)


def _const_spec(shape):
    return pl.BlockSpec(shape, lambda *_: (0,) * len(shape))


def _layer_spec(param, single_buffer=False):
    stack, layer = param
    shape = (None,) + stack.shape[1:]
    index_map = lambda *_: (layer,) + (0,) * (stack.ndim - 1)
    if single_buffer:
        return pl.BlockSpec(shape, index_map, pipeline_mode=pl.Buffered(1))
    return pl.BlockSpec(shape, index_map)


def _swiglu(x_ref, g_ref, win_ref, wout_ref, h_ref, after_chunk=None):
    u = _rms(x_ref[...], g_ref[...]).astype(BF16)
    for c in range(D_FF // FFN_CHUNK):
        lo = c * FFN_CHUNK
        a = _dot(u, win_ref[:, lo:lo + FFN_CHUNK])
        b = _dot(u, win_ref[:, D_FF + lo:D_FF + lo + FFN_CHUNK])
        h_ref[:, lo:lo + FFN_CHUNK] = (a * jax.nn.sigmoid(a) * b).astype(BF16)
        if after_chunk is not None:
            after_chunk(c)
    return x_ref[...] + 0.5 * _dot(h_ref[...], wout_ref[...])


def _pool_windows(x, t, hist_ref, g_ref, newhist_ref, ext_ref, pooled_ref, *, hist_valid):
    tile = x.shape[0]
    ext_ref[0:HIST_ROWS, :] = jnp.where(t == 0, hist_ref[0], ext_ref[0:HIST_ROWS, :])
    u = _rms(x, g_ref[...])
    ext_ref[HIST_ROWS:HIST_ROWS + tile, :] = u

    if not hist_valid:
        pos = t * tile + lax.broadcasted_iota(jnp.int32, (tile, 1), 0)

    for gi, w in enumerate(POOL_WINDOWS):
        cols = slice(gi * POOL_GROUP, (gi + 1) * POOL_GROUP)
        s = ext_ref[:, cols]
        k = 1
        while k < w:
            s = s + pltpu.roll(s, k, 0)
            k *= 2
        s = s[HIST_ROWS:, :]
        if hist_valid:
            mean = s * (1.0 / w)
        else:
            mean = s / jnp.minimum(pos + 1, w).astype(F32)
        pooled_ref[:, cols] = (mean - u[:, cols]).astype(BF16)

    last_rows = ext_ref[tile:tile + HIST_ROWS, :]
    ext_ref[0:HIST_ROWS, :] = last_rows
    newhist_ref[0] = last_rows


def _pool_project(gi, pooled_ref, w_ref, scale_ref, o_ref):
    cols = slice(gi * POOL_GROUP, (gi + 1) * POOL_GROUP)
    o_ref[:, cols] += _dot(pooled_ref[:, cols], w_ref[gi]) * scale_ref[:, cols]


def _ffn_body(x_ref, g_ref, win_ref, wout_ref, o_ref, h_ref):
    o_ref[...] = _swiglu(x_ref, g_ref, win_ref, wout_ref, h_ref)


def _ffn_norm_body(x_ref, g_ref, win_ref, wout_ref, fg_ref, o_ref, h_ref):
    o_ref[...] = _rms(_swiglu(x_ref, g_ref, win_ref, wout_ref, h_ref), fg_ref[...])


def _ffn_pool_body(x_ref, g_ref, win_ref, wout_ref, hist_ref, pg_ref, pw_ref, ps_ref,
                   o_ref, newhist_ref, h_ref, x1_ref, pooled_ref, ext_ref, *, tiles_per_stream):
    i = pl.program_id(0)

    @pl.when(i == 0)
    def _():
        x1_ref[...] = jnp.zeros_like(x1_ref)
        ext_ref[...] = jnp.zeros_like(ext_ref)

    t = lax.rem(i - 1, jnp.int32(tiles_per_stream))
    x1 = x1_ref[...]
    o_ref[...] = x1
    _pool_windows(x1, t, hist_ref, pg_ref, newhist_ref, ext_ref, pooled_ref, hist_valid=True)

    def project_behind(c):
        if c in POOL_PROJECT_AFTER_CHUNK:
            _pool_project(POOL_PROJECT_AFTER_CHUNK.index(c), pooled_ref, pw_ref, ps_ref, o_ref)

    x1_ref[...] = _swiglu(x_ref, g_ref, win_ref, wout_ref, h_ref, project_behind)


def _pool_body(x_ref, hist_ref, pg_ref, pw_ref, ps_ref, o_ref, newhist_ref, ext_ref, pooled_ref,
               *, hist_valid):
    x = x_ref[...]
    o_ref[...] = x
    _pool_windows(x, 0, hist_ref, pg_ref, newhist_ref, ext_ref, pooled_ref, hist_valid=hist_valid)
    for gi in range(len(POOL_WINDOWS)):
        _pool_project(gi, pooled_ref, pw_ref, ps_ref, o_ref)


def _ffn(x, gain, w_in, w_out, *, tile, final_gain=None, in_place=False):
    row_spec = pl.BlockSpec((tile, D_MODEL), lambda i: (i, 0))
    in_specs = [row_spec, _layer_spec(gain),
                _layer_spec(w_in, single_buffer=True),
                _layer_spec(w_out, single_buffer=True)]
    args = [x, gain[0], w_in[0], w_out[0]]
    body = _ffn_body
    if final_gain is not None:
        body = _ffn_norm_body
        in_specs.append(_layer_spec(final_gain))
        args.append(final_gain[0])
    return pl.pallas_call(
        body,
        grid=(x.shape[0] // tile,),
        in_specs=in_specs,
        out_specs=row_spec,
        out_shape=jax.ShapeDtypeStruct(x.shape, x.dtype),
        scratch_shapes=[pltpu.VMEM((tile, D_FF), BF16)],
        input_output_aliases={0: 0} if in_place else {},
        compiler_params=_params(1),
        name="ffn",
    )(*args)


def _ffn_pool(x, gain, w_in, w_out, hist, pool_gain, w_pool, scale, *, n_streams, tile):
    n = x.shape[0] // tile
    tiles_per_stream = n // n_streams
    row_spec = pl.BlockSpec((tile, D_MODEL), lambda i: (jnp.minimum(i, n - 1), 0))
    out_spec = pl.BlockSpec((tile, D_MODEL), lambda i: (jnp.maximum(i - 1, 0), 0))
    newhist_spec = pl.BlockSpec((1, HIST_ROWS, D_MODEL),
                                lambda i: (jnp.maximum(i - 1, 0) // tiles_per_stream, 0, 0))
    return pl.pallas_call(
        functools.partial(_ffn_pool_body, tiles_per_stream=tiles_per_stream),
        grid=(n + 1,),
        in_specs=[row_spec, _layer_spec(gain),
                  _layer_spec(w_in, single_buffer=True),
                  _layer_spec(w_out, single_buffer=True),
                  _const_spec((1, HIST_ROWS, D_MODEL)),
                  _layer_spec(pool_gain),
                  _layer_spec(w_pool),
                  _layer_spec(scale)],
        out_specs=[out_spec, newhist_spec],
        out_shape=[jax.ShapeDtypeStruct(x.shape, x.dtype),
                   jax.ShapeDtypeStruct((n_streams, HIST_ROWS, D_MODEL), F32)],
        scratch_shapes=[pltpu.VMEM((tile, D_FF), BF16),
                        pltpu.VMEM((tile, D_MODEL), F32),
                        pltpu.VMEM((tile, D_MODEL), BF16),
                        pltpu.VMEM((tile + HIST_ROWS, D_MODEL), F32)],
        compiler_params=_params(1),
        name="ffn_pool",
    )(x, gain[0], w_in[0], w_out[0], hist, pool_gain[0], w_pool[0], scale[0])


def _pool(x, hist, gain, w_pool, scale, *, row0, n_streams, length, hist_valid):
    blk0 = row0 // length
    row_spec = pl.BlockSpec((length, D_MODEL), lambda b: (blk0 + b, 0))
    hist_spec = pl.BlockSpec((1, HIST_ROWS, D_MODEL), lambda b: (b, 0, 0))
    return pl.pallas_call(
        functools.partial(_pool_body, hist_valid=hist_valid),
        grid=(n_streams,),
        in_specs=[row_spec, hist_spec, _layer_spec(gain), _layer_spec(w_pool),
                  _layer_spec(scale)],
        out_specs=[row_spec, hist_spec],
        out_shape=[jax.ShapeDtypeStruct(x.shape, x.dtype),
                   jax.ShapeDtypeStruct((n_streams, HIST_ROWS, D_MODEL), F32)],
        scratch_shapes=[pltpu.VMEM((length + HIST_ROWS, D_MODEL), F32),
                        pltpu.VMEM((length, D_MODEL), BF16)],
        input_output_aliases={0: 0},
        compiler_params=_params(1),
        name="pool",
    )(x, hist, gain[0], w_pool[0], scale[0])


def _ret_body(x_ref, cos_ref, sin_ref, s0_ref, g_ref, win_ref, wout_ref, ng_ref,
              o_ref, sfin_ref,
              qd_ref, kd_ref, v_ref, gate_ref, og_ref, s_ref, dq_ref, dk_ref,
              *, tile, chunk):
    b = pl.program_id(0)
    t = pl.program_id(1)
    n_chunks = tile // chunk

    @pl.when((b == 0) & (t == 0))
    def _():
        steps = (lax.broadcasted_iota(jnp.int32, (chunk, ROPE_HALF), 0) + 1).astype(F32)
        for h in range(RET_HEADS):
            dq_ref[h] = jnp.exp(steps * LOG_GAMMA[h])
            dk_ref[h] = jnp.exp(steps * (-LOG_GAMMA[h])) * (RET_DK ** -0.5)

    @pl.when(t == 0)
    def _():
        s_ref[...] = s0_ref[0]

    u = _rms(x_ref[...], g_ref[...]).astype(BF16)
    cos = cos_ref[...]
    sin = sin_ref[...]

    def rotate(p):
        x1 = p[:, :ROPE_HALF]
        x2 = p[:, ROPE_HALF:]
        return x1 * cos - x2 * sin, x1 * sin + x2 * cos

    for h in range(RET_HEADS):
        lo = h * RET_DK
        for off, dec_ref, dst_ref in ((0, dq_ref, qd_ref), (D_MODEL, dk_ref, kd_ref)):
            r1, r2 = rotate(_dot(u, win_ref[:, off + lo:off + lo + RET_DK]))
            dec = dec_ref[h]
            for ci in range(n_chunks):
                rows = slice(ci * chunk, (ci + 1) * chunk)
                dst_ref[rows, lo:lo + ROPE_HALF] = (r1[rows] * dec).astype(BF16)
                dst_ref[rows, lo + ROPE_HALF:lo + RET_DK] = (r2[rows] * dec).astype(BF16)

    for c in range(RET_VDIM // PROJ_CHUNK):
        lo = c * PROJ_CHUNK
        v_ref[:, lo:lo + PROJ_CHUNK] = _dot(
            u, win_ref[:, 2 * D_MODEL + lo:2 * D_MODEL + lo + PROJ_CHUNK]).astype(BF16)
        gp = _dot(u, win_ref[:, 2 * D_MODEL + RET_VDIM + lo:2 * D_MODEL + RET_VDIM + lo + PROJ_CHUNK])
        gate_ref[:, lo:lo + PROJ_CHUNK] = (gp * jax.nn.sigmoid(gp)).astype(BF16)

    causal = (lax.broadcasted_iota(jnp.int32, (chunk, chunk), 0)
              >= lax.broadcasted_iota(jnp.int32, (chunk, chunk), 1))
    for ci in range(n_chunks):
        rows = slice(ci * chunk, (ci + 1) * chunk)
        for h in range(RET_HEADS):
            kcols = slice(h * RET_DK, (h + 1) * RET_DK)
            vcols = slice(h * RET_DV, (h + 1) * RET_DV)
            qd = qd_ref[rows, kcols]
            kd = kd_ref[rows, kcols]
            v = v_ref[rows, vcols]
            scores = lax.dot_general(qd, kd, (((1,), (1,)), ((), ())), preferred_element_type=F32)
            p = jnp.where(causal, scores, 0.0).astype(BF16)
            state = s_ref[h]
            o = _dot(p, v) + _dot(qd, state.astype(BF16))
            kv = lax.dot_general(kd, v, (((0,), (0,)), ((), ())), preferred_element_type=F32)
            s_ref[h] = math.exp(chunk * LOG_GAMMA[h]) * (state + kv)
            on = o * lax.rsqrt(jnp.mean(o * o, axis=-1, keepdims=True) + EPS) * ng_ref[:, vcols]
            og_ref[rows, vcols] = (gate_ref[rows, vcols].astype(F32) * on).astype(BF16)

    o_ref[...] = x_ref[...] + _dot(og_ref[...], wout_ref[...])

    @pl.when(t == pl.num_programs(1) - 1)
    def _():
        sfin_ref[0] = s_ref[...]


def _ret_update_body(x_ref, cos_ref, sin_ref, s0_ref, g_ref, win_ref, wout_ref, ng_ref, states_ref,
                     *rest, tile, chunk):
    del states_ref
    _ret_body(x_ref, cos_ref, sin_ref, s0_ref, g_ref, win_ref, wout_ref, ng_ref, *rest,
              tile=tile, chunk=chunk)


def _ret(x, cos, sin, s0, gain, w_in, w_out, norm_gain, *, row0, n_streams, length, tile, chunk,
         in_place, states=None, n_state_layers=1, state_layer=0):
    tiles = length // tile
    blk0 = row0 // tile
    state_block = (None, 1, RET_HEADS, RET_DK, RET_DV)
    s0_stack, s0_layer = s0
    if s0_stack.shape[1] == 1:
        s0_map = lambda b, t: (s0_layer, 0, 0, 0, 0)
    else:
        s0_map = lambda b, t: (s0_layer, b, 0, 0, 0)
    row_spec = pl.BlockSpec((tile, D_MODEL), lambda b, t: (blk0 + b * tiles + t, 0))
    rope_spec = pl.BlockSpec((tile, ROPE_HALF), lambda b, t: (t, 0))
    in_specs = [row_spec, rope_spec, rope_spec,
                pl.BlockSpec(state_block, s0_map),
                _layer_spec(gain),
                _layer_spec(w_in, single_buffer=True),
                _layer_spec(w_out, single_buffer=True),
                _layer_spec(norm_gain)]
    args = [x, cos, sin, s0_stack, gain[0], w_in[0], w_out[0], norm_gain[0]]
    aliases = {0: 0} if in_place else {}
    body = _ret_body
    if states is not None:
        body = _ret_update_body
        aliases[len(args)] = 1
        in_specs.append(pl.BlockSpec(memory_space=pl.ANY))
        args.append(states)
    return pl.pallas_call(
        functools.partial(body, tile=tile, chunk=chunk),
        grid=(n_streams, tiles),
        in_specs=in_specs,
        out_specs=[row_spec, pl.BlockSpec(state_block, lambda b, t: (state_layer, b, 0, 0, 0))],
        out_shape=[jax.ShapeDtypeStruct(x.shape, x.dtype),
                   jax.ShapeDtypeStruct((n_state_layers, n_streams) + state_block[2:], F32)],
        scratch_shapes=[
            pltpu.VMEM((tile, D_MODEL), BF16),
            pltpu.VMEM((tile, D_MODEL), BF16),
            pltpu.VMEM((tile, RET_VDIM), BF16),
            pltpu.VMEM((tile, RET_VDIM), BF16),
            pltpu.VMEM((tile, RET_VDIM), BF16),
            pltpu.VMEM((RET_HEADS, RET_DK, RET_DV), F32),
            pltpu.VMEM((RET_HEADS, chunk, ROPE_HALF), F32),
            pltpu.VMEM((RET_HEADS, chunk, ROPE_HALF), F32),
        ],
        input_output_aliases=aliases,
        compiler_params=_params(2),
        name="ret",
    )(*args)


def _rope_tables(pos0, length):
    inv = ROPE_BASE ** (-jnp.arange(ROPE_HALF, dtype=jnp.float32) / ROPE_HALF)
    ang = (pos0 + jnp.arange(length)).astype(jnp.float32)[:, None] * inv[None, :]
    return jnp.cos(ang).astype(F32), jnp.sin(ang).astype(F32)


def kernel(x_prompt, x_sample, cache_pool, state_ret, meta_tokens, norm_g, final_norm_g, w_ffn1_in, w_ffn1_out, w_ffn2_in, w_ffn2_out, w_pool, pool_scale, w_ret_in, w_ret_out, ret_norm_g):
    batch, seq, _ = x_prompt.shape
    dec_batch, dec_seq, _ = x_sample.shape
    depth = norm_g.shape[0]
    n_ret = state_ret.shape[0]
    assert meta_tokens.shape[0] == N_META_ROWS
    assert seq % FFN_TILE == 0 and seq % FFN_POOL_TILE == 0 and seq % RET_TILE == 0

    meta_row0 = dec_batch * dec_seq
    small_rows = meta_row0 + N_META_ROWS
    assert meta_row0 % N_META_ROWS == 0
    xs = jnp.concatenate([x_sample.reshape(meta_row0, D_MODEL),
                          meta_tokens.astype(x_sample.dtype)], axis=0)
    xp = x_prompt.reshape(batch * seq, D_MODEL)

    gains = norm_g.astype(F32).reshape(depth * 3, 1, D_MODEL)
    final_gain = (final_norm_g.astype(F32).reshape(1, 1, D_MODEL), 0)
    ffn1_in, ffn1_out = w_ffn1_in.astype(BF16), w_ffn1_out.astype(BF16)
    ffn2_in, ffn2_out = w_ffn2_in.astype(BF16), w_ffn2_out.astype(BF16)
    pool_w = w_pool.astype(BF16)
    pool_sc = pool_scale.astype(F32).reshape(-1, 1, D_MODEL)
    ret_in, ret_out = w_ret_in.astype(BF16), w_ret_out.astype(BF16)
    ret_ng = ret_norm_g.astype(F32).reshape(-1, 1, RET_VDIM)
    state_in = state_ret.astype(F32)

    rope_meta = _rope_tables(0, N_META_ROWS)
    rope_prompt = _rope_tables(N_META_ROWS, seq)
    rope_sample = _rope_tables(N_META_ROWS + PAST_LEN, dec_seq)
    zero_hist = jnp.zeros((1, HIST_ROWS, D_MODEL), F32)
    zero_state = jnp.zeros((1, 1, RET_HEADS, RET_DK, RET_DV), F32)

    pool_p, pool_s = [], []
    states_p = states_s = None
    for i in range(depth):
        w1 = ((gains, 3 * i), (ffn1_in, i), (ffn1_out, i))
        mix_gain = (gains, 3 * i + 1)
        w2 = ((gains, 3 * i + 2), (ffn2_in, i), (ffn2_out, i))
        last_gain = final_gain if i == depth - 1 else None
        j = i // 2
        xs = _ffn(xs, *w1, tile=small_rows, in_place=True)
        if i % 2 == 0:
            wp, sc = (pool_w, j), (pool_sc, j)
            cache = jnp.pad(cache_pool[j].astype(F32), ((0, 0), (1, 0), (0, 0)))
            xs, hist_meta = _pool(xs, zero_hist, mix_gain, wp, sc, row0=meta_row0, n_streams=1,
                                  length=N_META_ROWS, hist_valid=False)
            xs, hist_s = _pool(xs, cache, mix_gain, wp, sc, row0=0, n_streams=dec_batch,
                               length=dec_seq, hist_valid=True)
            xp, hist_p = _ffn_pool(xp, *w1, hist_meta, mix_gain, wp, sc, n_streams=batch,
                                   tile=FFN_POOL_TILE)
            pool_p.append(hist_p[:, 1:])
            pool_s.append(hist_s[:, 1:])
        else:
            wr = (mix_gain, (ret_in, j), (ret_out, j), (ret_ng, j))
            xs, state_meta = _ret(xs, *rope_meta, (zero_state, 0), *wr, row0=meta_row0,
                                  n_streams=1, length=N_META_ROWS, tile=N_META_ROWS,
                                  chunk=N_META_ROWS, in_place=True)
            xs, states_s = _ret(xs, *rope_sample, (state_in, j), *wr, row0=0,
                                n_streams=dec_batch, length=dec_seq, tile=dec_seq, chunk=dec_seq,
                                in_place=True, states=states_s, n_state_layers=n_ret,
                                state_layer=j)
            xp = _ffn(xp, *w1, tile=FFN_TILE)
            xp, states_p = _ret(xp, *rope_prompt, (state_meta, 0), *wr, row0=0, n_streams=batch,
                                length=seq, tile=RET_TILE, chunk=RET_CHUNK, in_place=False,
                                states=states_p, n_state_layers=n_ret, state_layer=j)
        xs = _ffn(xs, *w2, tile=small_rows, final_gain=last_gain, in_place=True)
        xp = _ffn(xp, *w2, tile=FFN_TILE, final_gain=last_gain)

    return (xp.reshape(batch, seq, D_MODEL),
            xs[:meta_row0].reshape(dec_batch, dec_seq, D_MODEL),
            jnp.stack(pool_p), jnp.stack(pool_s), states_p, states_s)
```
